```python
import math
import jax, jax.numpy as jnp
from jax import lax
import numpy as np


D_MODEL = 1024
BATCH = 16
SEQ = 4096
DEPTH = 4
DEC_BATCH = 2
DEC_SEQ = 16384
PAST_LEN = 128

N_MIXERS = 3
GRID_W = 64
HEAD_DIM = 64
D_FF = 2816
EPS = 1e-6
Q_BLOCK = 128
NEG_INF = -1e30
A_HEADS = D_MODEL // (2 * HEAD_DIM)
A_VDIM = 2 * HEAD_DIM
T5_BUCKETS = 32
T5_MAX_DIST = 128
B_HEADS = D_MODEL // HEAD_DIM
B_KV_HEADS = B_HEADS // 4
B_GROUP = B_HEADS // B_KV_HEADS
ROPE_THETA = 10000.0
C_HEADS = D_MODEL // HEAD_DIM
NA_ROWS_MAX = 8
NA_COLS = 16

kernel_name = 'hybrid_bidir_encoder_interleaved'


def _layer_counts():
    n_a = len(range(0, DEPTH, N_MIXERS))
    n_b = len(range(1, DEPTH, N_MIXERS))
    n_c = len(range(2, DEPTH, N_MIXERS))
    return n_a, n_b, n_c


def rms_norm(x, g):
    xf = x.astype(jnp.float32)
    y = xf * lax.rsqrt(jnp.mean(xf * xf, axis=-1, keepdims=True) + EPS)
    return (y * g.astype(jnp.float32)).astype(x.dtype)


def swiglu(x, w_in, w_out):
    gate, up = jnp.split(x @ w_in, 2, axis=-1)
    return (jax.nn.silu(gate) * up) @ w_out


def t5_bucket(rel):
    nb = T5_BUCKETS // 2
    max_exact = nb // 2
    ret = jnp.where(rel > 0, nb, 0)
    n = jnp.abs(rel)
    n_f = jnp.maximum(n, 1).astype(jnp.float32)
    large = max_exact + (jnp.log(n_f / max_exact) / math.log(T5_MAX_DIST / max_exact) * (nb - max_exact)).astype(jnp.int32)
    large = jnp.minimum(large, nb - 1)
    return ret + jnp.where(n < max_exact, n, large)


def diff_attention(h, w_qkv, lam, subln, w_out, t5_table, lambda_init):
    b, s_len, _ = h.shape
    q, k, v = jnp.split(h @ w_qkv, 3, axis=-1)
    nblk = s_len // Q_BLOCK
    q = q.reshape(b, nblk, Q_BLOCK, A_HEADS, 2, HEAD_DIM).transpose(1, 0, 2, 3, 4, 5)
    k = k.reshape(b, s_len, A_HEADS, 2, HEAD_DIM)
    v = v.reshape(b, s_len, A_HEADS, A_VDIM)
    lamf = lam.astype(jnp.float32)
    lam_full = jnp.exp(jnp.sum(lamf[0] * lamf[1])) - jnp.exp(jnp.sum(lamf[2] * lamf[3])) + lambda_init
    offs = jnp.arange(-(s_len - 1), s_len, dtype=jnp.int32)
    bias_vec = t5_table.astype(jnp.float32)[t5_bucket(offs)]
    kpos = jnp.arange(s_len, dtype=jnp.int32)
    scale = HEAD_DIM ** -0.5

    def block(args):
        qblk, start = args
        sc = jnp.einsum('bqhid,bkhid->bhiqk', qblk, k).astype(jnp.float32) * scale
        qpos = start + jnp.arange(Q_BLOCK, dtype=jnp.int32)
        bias = bias_vec[kpos[None, :] - qpos[:, None] + (s_len - 1)]
        sc = sc + jnp.transpose(bias, (2, 0, 1))[:, None]
        p = jax.nn.softmax(sc, axis=-1)
        attn = p[:, :, 0] - lam_full * p[:, :, 1]
        return jnp.einsum('bhqk,bkhe->bqhe', attn.astype(v.dtype), v)

    o = lax.map(block, (q, jnp.arange(nblk, dtype=jnp.int32) * Q_BLOCK))
    o = o.transpose(1, 0, 2, 3, 4).reshape(b, s_len, A_HEADS, A_VDIM)
    o = rms_norm(o, subln) * (1.0 - lambda_init)
    return o.reshape(b, s_len, D_MODEL) @ w_out


def _rope_half(x, pos):
    n = x.shape[-1] // 2
    freqs = ROPE_THETA ** (-jnp.arange(n, dtype=jnp.float32) / n)
    ang = pos.astype(jnp.float32)[:, None] * freqs[None, :]
    cos = jnp.cos(ang)[None, :, None, :]
    sin = jnp.sin(ang)[None, :, None, :]
    x1, x2 = x[..., :n], x[..., n:]
    return jnp.concatenate([x1 * cos - x2 * sin, x1 * sin + x2 * cos], axis=-1)


def axial_rope(x, row, col):
    xf = x.astype(jnp.float32)
    half = HEAD_DIM // 2
    return jnp.concatenate([_rope_half(xf[..., :half], row), _rope_half(xf[..., half:], col)], axis=-1).astype(x.dtype)


def gqa_axial(h, w_qkv, q_norm, k_norm, w_out):
    b, s_len, _ = h.shape
    qkv = h @ w_qkv
    nq = B_HEADS * HEAD_DIM
    nk = B_KV_HEADS * HEAD_DIM
    q = qkv[..., :nq].reshape(b, s_len, B_HEADS, HEAD_DIM)
    k = qkv[..., nq:nq + nk].reshape(b, s_len, B_KV_HEADS, HEAD_DIM)
    v = qkv[..., nq + nk:].reshape(b, s_len, B_KV_HEADS, HEAD_DIM)
    t = jnp.arange(s_len, dtype=jnp.int32)
    row, col = t // GRID_W, t % GRID_W
    q = axial_rope(rms_norm(q, q_norm), row, col)
    k = axial_rope(rms_norm(k, k_norm), row, col)
    nblk = s_len // Q_BLOCK
    q = q.reshape(b, nblk, Q_BLOCK, B_KV_HEADS, B_GROUP, HEAD_DIM).transpose(1, 0, 2, 3, 4, 5)
    scale = HEAD_DIM ** -0.5

    def block(qblk):
        sc = jnp.einsum('bqgrd,bkgd->bgrqk', qblk, k).astype(jnp.float32) * scale
        p = jax.nn.softmax(sc, axis=-1)
        return jnp.einsum('bgrqk,bkgd->bqgrd', p.astype(v.dtype), v)

    o = lax.map(block, q)
    o = o.transpose(1, 0, 2, 3, 4, 5).reshape(b, s_len, D_MODEL)
    return o @ w_out


def neighbourhood_attention(h, w_qkv, rpb, w_out):
    b, s_len, _ = h.shape
    rows = s_len // GRID_W
    kr = min(NA_ROWS_MAX, rows)
    q, k, v = jnp.split(h @ w_qkv, 3, axis=-1)
    q = q.reshape(b, rows, GRID_W, C_HEADS, HEAD_DIM)
    k = k.reshape(b, rows, GRID_W, C_HEADS, HEAD_DIM)
    v = v.reshape(b, rows, GRID_W, C_HEADS, HEAD_DIM)
    col = jnp.arange(GRID_W, dtype=jnp.int32)
    c_start = jnp.clip(col - NA_COLS // 2, 0, GRID_W - NA_COLS)
    col_mask = (col[None, :] >= c_start[:, None]) & (col[None, :] < c_start[:, None] + NA_COLS)
    col_idx = jnp.clip(col[None, :] - col[:, None] + NA_COLS - 1, 0, 2 * NA_COLS - 2)
    rpbf = rpb.astype(jnp.float32)
    scale = HEAD_DIM ** -0.5

    def row_step(r):
        rs = jnp.clip(r - kr // 2, 0, rows - kr)
        kw = lax.dynamic_slice_in_dim(k, rs, kr, axis=1)
        vw = lax.dynamic_slice_in_dim(v, rs, kr, axis=1)
        qr = lax.dynamic_index_in_dim(q, r, axis=1, keepdims=False)
        sc = jnp.einsum('bqhd,bikhd->bhqik', qr, kw).astype(jnp.float32) * scale
        row_idx = rs + jnp.arange(kr, dtype=jnp.int32) - r + NA_ROWS_MAX - 1
        bias = rpbf[:, row_idx][:, :, col_idx]
        sc = sc + jnp.transpose(bias, (0, 2, 1, 3))[None]
        sc = jnp.where(col_mask[None, None, :, None, :], sc, NEG_INF)
        p = jax.nn.softmax(sc.reshape(b, C_HEADS, GRID_W, kr * GRID_W), axis=-1)
        p = p.reshape(b, C_HEADS, GRID_W, kr, GRID_W)
        return jnp.einsum('bhqik,bikhd->bqhd', p.astype(vw.dtype), vw)

    o = lax.map(row_step, jnp.arange(rows, dtype=jnp.int32))
    o = o.transpose(1, 0, 2, 3, 4).reshape(b, s_len, D_MODEL)
    return o @ w_out


def encoder_trunk(x, ffn_norm, ffn_w_in, ffn_w_out, mix_norm, a_w_qkv, a_lambda, a_subln, a_w_out,
                  t5_table, b_w_qkv, b_q_norm, b_k_norm, b_w_out, c_w_qkv, c_rpb, c_w_out, final_norm):
    for i in range(DEPTH):
        j = i // N_MIXERS
        kind = i % N_MIXERS
        x = x + 0.5 * swiglu(rms_norm(x, ffn_norm[i, 0]), ffn_w_in[i, 0], ffn_w_out[i, 0])
        h = rms_norm(x, mix_norm[i])
        if kind == 0:
            lambda_init = 0.8 - 0.6 * math.exp(-0.3 * i)
            m = diff_attention(h, a_w_qkv[j], a_lambda[j], a_subln[j], a_w_out[j], t5_table, lambda_init)
        elif kind == 1:
            m = gqa_axial(h, b_w_qkv[j], b_q_norm[j], b_k_norm[j], b_w_out[j])
        else:
            m = neighbourhood_attention(h, c_w_qkv[j], c_rpb[j], c_w_out[j])
        x = x + m
        x = x + 0.5 * swiglu(rms_norm(x, ffn_norm[i, 1]), ffn_w_in[i, 1], ffn_w_out[i, 1])
    return rms_norm(x, final_norm)


def setup_inputs(seed: int = 0) -> dict:
    key = jax.random.key(seed)
    ks = jax.random.split(key, 24)
    n_a, n_b, n_c = _layer_counts()
    f32 = jnp.float32
    def w(k, shape, fan_in):
        return jax.random.normal(k, shape, f32) * (fan_in ** -0.5)
    def gain(k, shape):
        return 1.0 + 0.01 * jax.random.normal(k, shape, f32)
    return {
        'x_prompt': jax.random.normal(ks[0], (BATCH, SEQ, D_MODEL), f32),
        'x_sample': jax.random.normal(ks[1], (DEC_BATCH, DEC_SEQ, D_MODEL), f32),
        'ffn_norm': gain(ks[2], (DEPTH, 2, D_MODEL)),
        'ffn_w_in': w(ks[3], (DEPTH, 2, D_MODEL, 2 * D_FF), D_MODEL),
        'ffn_w_out': w(ks[4], (DEPTH, 2, D_FF, D_MODEL), D_FF),
        'mix_norm': gain(ks[5], (DEPTH, D_MODEL)),
        'a_w_qkv': w(ks[6], (n_a, D_MODEL, 3 * D_MODEL), D_MODEL),
        'a_lambda': 0.1 * jax.random.normal(ks[7], (n_a, 4, HEAD_DIM), f32),
        'a_subln': gain(ks[8], (n_a, A_VDIM)),
        'a_w_out': w(ks[9], (n_a, D_MODEL, D_MODEL), D_MODEL),
        't5_table': 0.1 * jax.random.normal(ks[10], (T5_BUCKETS, A_HEADS), f32),
        'b_w_qkv': w(ks[11], (n_b, D_MODEL, (B_HEADS + 2 * B_KV_HEADS) * HEAD_DIM), D_MODEL),
        'b_q_norm': gain(ks[12], (n_b, HEAD_DIM)),
        'b_k_norm': gain(ks[13], (n_b, HEAD_DIM)),
        'b_w_out': w(ks[14], (n_b, D_MODEL, D_MODEL), D_MODEL),
        'c_w_qkv': w(ks[15], (n_c, D_MODEL, 3 * D_MODEL), D_MODEL),
        'c_rpb': 0.1 * jax.random.normal(ks[16], (n_c, C_HEADS, 2 * NA_ROWS_MAX - 1, 2 * NA_COLS - 1), f32),
        'c_w_out': w(ks[17], (n_c, D_MODEL, D_MODEL), D_MODEL),
        'final_norm': gain(ks[18], (D_MODEL,)),
    }


def reference(x_prompt, x_sample, ffn_norm, ffn_w_in, ffn_w_out, mix_norm, a_w_qkv, a_lambda, a_subln, a_w_out,
              t5_table, b_w_qkv, b_q_norm, b_k_norm, b_w_out, c_w_qkv, c_rpb, c_w_out, final_norm):
    y_prompt = encoder_trunk(x_prompt, ffn_norm, ffn_w_in, ffn_w_out, mix_norm, a_w_qkv, a_lambda, a_subln, a_w_out,
                             t5_table, b_w_qkv, b_q_norm, b_k_norm, b_w_out, c_w_qkv, c_rpb, c_w_out, final_norm)
    y_sample = encoder_trunk(x_sample, ffn_norm, ffn_w_in, ffn_w_out, mix_norm, a_w_qkv, a_lambda, a_subln, a_w_out,
                             t5_table, b_w_qkv, b_q_norm, b_k_norm, b_w_out, c_w_qkv, c_rpb, c_w_out, final_norm)
    return (y_prompt, y_sample)
```

```python
import functools
import math

import jax
import jax.numpy as jnp
from jax import lax
from jax.experimental import pallas as pl
from jax.experimental.pallas import tpu as pltpu

F32 = jnp.float32
BF16 = jnp.bfloat16

EPS = 1e-6
HEAD_DIM = 64
GRID_W = 64
NEG_INF = -1e30
N_MIXERS = 3
T5_BUCKETS = 32
T5_MAX_DIST = 128
ROPE_THETA = 10000.0
NA_ROWS = 8
NA_COLS = 16

V7X_VMEM_BYTES = 64 * 1024 * 1024
VMEM_LIMIT_BYTES = V7X_VMEM_BYTES - 8 * 1024 * 1024
MXU_DIM = 256
ROW_TILE = 512
FFN_CHUNK = MXU_DIM
ATTN_A_TILE = 256
ATTN_B_TQ = 128
ATTN_B_TK = 512
NA_Q_ROWS = 8
NA_K_ROWS = 16


def _params(*sem):
    return pltpu.CompilerParams(dimension_semantics=sem, vmem_limit_bytes=VMEM_LIMIT_BYTES)


def _const_spec(shape):
    nd = len(shape)
    return pl.BlockSpec(shape, lambda *_: (0,) * nd, pipeline_mode=pl.Buffered(1))


def _rms(x, gain):
    ms = jnp.mean(x * x, axis=-1, keepdims=True)
    return x * lax.rsqrt(ms + EPS) * gain


def _ffn_kernel(x_ref, g_ref, wi_ref, wo_ref, fg_ref, o_ref, act_ref, *, d_ff, final):
    x = x_ref[...]
    xn = _rms(x, g_ref[...]).astype(BF16)
    for c in range(d_ff // FFN_CHUNK):
        lo = c * FFN_CHUNK
        gate = jnp.dot(xn, wi_ref[:, lo:lo + FFN_CHUNK], preferred_element_type=F32)
        up = jnp.dot(xn, wi_ref[:, d_ff + lo:d_ff + lo + FFN_CHUNK], preferred_element_type=F32)
        act_ref[:, lo:lo + FFN_CHUNK] = (gate * jax.nn.sigmoid(gate) * up).astype(BF16)
    y = x + 0.5 * jnp.dot(act_ref[...], wo_ref[...], preferred_element_type=F32)
    if final:
        y = _rms(y, fg_ref[...])
    o_ref[...] = y


def _ffn(x, gain, w_in, w_out, final_gain, final):
    t, d = x.shape
    d_ff = w_out.shape[0]
    row = pl.BlockSpec((ROW_TILE, d), lambda i: (i, 0))
    return pl.pallas_call(
        functools.partial(_ffn_kernel, d_ff=d_ff, final=final),
        grid=(t // ROW_TILE,),
        in_specs=[row, _const_spec((1, d)), _const_spec(w_in.shape), _const_spec(w_out.shape),
                  _const_spec((1, d))],
        out_specs=row,
        out_shape=jax.ShapeDtypeStruct((t, d), F32),
        scratch_shapes=[pltpu.VMEM((ROW_TILE, d_ff), BF16)],
        compiler_params=_params("parallel"),
        name="ffn",
    )(x, gain, w_in, w_out, final_gain)


def _dot_nt(a, b):
    return lax.dot_general(a, b, (((1,), (1,)), ((), ())), preferred_element_type=F32)


def _proj_kernel(x_ref, g_ref, wqt_ref, wk_ref, wvt_ref, qt_ref, k_ref, vt_ref):
    hn = _rms(x_ref[...], g_ref[...]).astype(BF16)
    qt_ref[...] = (_dot_nt(wqt_ref[...], hn) * (HEAD_DIM ** -0.5)).astype(BF16)
    k_ref[...] = jnp.dot(hn, wk_ref[...], preferred_element_type=F32).astype(BF16)
    vt_ref[...] = _dot_nt(wvt_ref[...], hn).astype(BF16)


def _proj(x, gain, wqt, wk, wvt):
    t, d = x.shape
    row = pl.BlockSpec((ROW_TILE, d), lambda i: (i, 0))
    col = pl.BlockSpec((d, ROW_TILE), lambda i: (0, i))
    return pl.pallas_call(
        _proj_kernel,
        grid=(t // ROW_TILE,),
        in_specs=[row, _const_spec((1, d)), _const_spec(wqt.shape), _const_spec(wk.shape),
                  _const_spec(wvt.shape)],
        out_specs=[col, row, col],
        out_shape=[jax.ShapeDtypeStruct((d, t), BF16), jax.ShapeDtypeStruct((t, d), BF16),
                   jax.ShapeDtypeStruct((d, t), BF16)],
        compiler_params=_params("parallel"),
        name="proj_qkv",
    )(x, gain, wqt, wk, wvt)


def _norm_rope_t(yt, gain, cos, sin, n_heads):
    q4 = HEAD_DIM // 4
    out = []
    for h in range(n_heads):
        y = yt[h * HEAD_DIM:(h + 1) * HEAD_DIM]
        ms = jnp.mean(y * y, axis=0, keepdims=True)
        yn = y * lax.rsqrt(ms + EPS) * gain
        swapped = jnp.concatenate([yn[q4:2 * q4], yn[:q4], yn[3 * q4:], yn[2 * q4:3 * q4]], axis=0)
        out.append(yn * cos + swapped * sin)
    return jnp.concatenate(out, axis=0)


def _proj_b_kernel(x_ref, g_ref, wqt_ref, wkt_ref, wvt_ref, qn_ref, kn_ref, cos_ref, sin_ref,
                   qt_ref, k_ref, vt_ref, *, n_q, n_kv):
    hn = _rms(x_ref[...], g_ref[...]).astype(BF16)
    cos = cos_ref[...]
    sin = sin_ref[...]
    q = _norm_rope_t(_dot_nt(wqt_ref[...], hn), qn_ref[...], cos, sin, n_q)
    qt_ref[...] = (q * (HEAD_DIM ** -0.5)).astype(BF16)
    k = _norm_rope_t(_dot_nt(wkt_ref[...], hn), kn_ref[...], cos, sin, n_kv)
    k_ref[...] = k.T.astype(BF16)
    vt_ref[...] = _dot_nt(wvt_ref[...], hn).astype(BF16)


def _proj_b(x, gain, wqt, wkt, wvt, q_norm, k_norm, cos_t, sin_t):
    t, d = x.shape
    dq, dkv = wqt.shape[0], wkt.shape[0]
    s_tiles = cos_t.shape[1] // ROW_TILE
    row = pl.BlockSpec((ROW_TILE, d), lambda i: (i, 0))
    tab = pl.BlockSpec((HEAD_DIM, ROW_TILE), lambda i: (0, i % s_tiles))
    return pl.pallas_call(
        functools.partial(_proj_b_kernel, n_q=dq // HEAD_DIM, n_kv=dkv // HEAD_DIM),
        grid=(t // ROW_TILE,),
        in_specs=[row, _const_spec((1, d)), _const_spec(wqt.shape), _const_spec(wkt.shape),
                  _const_spec(wvt.shape), _const_spec((HEAD_DIM, 1)), _const_spec((HEAD_DIM, 1)),
                  tab, tab],
        out_specs=[pl.BlockSpec((dq, ROW_TILE), lambda i: (0, i)),
                   pl.BlockSpec((ROW_TILE, dkv), lambda i: (i, 0)),
                   pl.BlockSpec((dkv, ROW_TILE), lambda i: (0, i))],
        out_shape=[jax.ShapeDtypeStruct((dq, t), BF16), jax.ShapeDtypeStruct((t, dkv), BF16),
                   jax.ShapeDtypeStruct((dkv, t), BF16)],
        compiler_params=_params("parallel"),
        name="proj_qkv_rope",
    )(x, gain, wqt, wkt, wvt, q_norm, k_norm, cos_t, sin_t)


def _out_proj_kernel(x_ref, o_ref, w_ref, y_ref):
    y_ref[...] = x_ref[...] + jnp.dot(o_ref[...], w_ref[...], preferred_element_type=F32)


def _out_proj(x, o, w):
    t, d = x.shape
    row = pl.BlockSpec((ROW_TILE, d), lambda i: (i, 0))
    return pl.pallas_call(
        _out_proj_kernel,
        grid=(t // ROW_TILE,),
        in_specs=[row, row, _const_spec(w.shape)],
        out_specs=row,
        out_shape=jax.ShapeDtypeStruct((t, d), F32),
        compiler_params=_params("parallel"),
        name="out_proj",
    )(x, o, w)


def _softmax_step(s, shift_const, vt_tile, m_ref, l_ref, acc_ref):
    m_prev = m_ref[...]
    s_max = jnp.max(s, axis=0, keepdims=True)
    if shift_const is not None:
        m_new = jnp.maximum(m_prev, s_max + shift_const)
        p = jnp.exp(s - (m_new - shift_const))
    else:
        m_new = jnp.maximum(m_prev, s_max)
        p = jnp.exp(s - m_new)
    alpha = jnp.exp(m_prev - m_new)
    l_ref[...] = alpha * l_ref[...] + jnp.sum(p, axis=0, keepdims=True)
    acc_ref[...] = alpha * acc_ref[...] + jnp.dot(vt_tile, p.astype(BF16), preferred_element_type=F32)
    m_ref[...] = m_new


def _reset(m_ref, l_ref, acc_ref):
    m_ref[...] = jnp.full(m_ref.shape, NEG_INF, F32)
    l_ref[...] = jnp.zeros(l_ref.shape, F32)
    acc_ref[...] = jnp.zeros(acc_ref.shape, F32)


def _attn_a_kernel(lam_ref, far_ref, qt_ref, k_ref, vt_ref, bias_ref, subln_ref, o_ref,
                   m_ref, l_ref, acc_ref, *, seq, tile, out_scale):
    head = pl.program_id(0)
    n_tiles = seq // tile
    lam = lam_ref[0]
    c_left = far_ref[head, 0]
    c_right = far_ref[head, 1]
    first_half = lax.broadcasted_iota(jnp.int32, (2 * HEAD_DIM, tile), 0) < HEAD_DIM

    def q_body(qi, carry):
        q0 = pl.multiple_of(qi * tile, tile)
        q_tile = qt_ref[:, pl.ds(q0, tile)]
        zeros = jnp.zeros_like(q_tile)
        q_bd = jnp.concatenate([jnp.where(first_half, q_tile, zeros),
                                jnp.where(first_half, zeros, q_tile)], axis=1)
        _reset(m_ref, l_ref, acc_ref)

        def step(ki, bias_tile, shift_const):
            k0 = pl.multiple_of(ki * tile, tile)
            s = jnp.dot(k_ref[pl.ds(k0, tile), :], q_bd, preferred_element_type=F32)
            if bias_tile is not None:
                s = s + jnp.concatenate([bias_tile, bias_tile], axis=1)
            _softmax_step(s, shift_const, vt_ref[:, pl.ds(k0, tile)], m_ref, l_ref, acc_ref)

        def far_left(ki, c):
            step(ki, None, c_left)
            return c

        def far_right(ki, c):
            step(ki, None, c_right)
            return c

        lax.fori_loop(0, qi - 1, far_left, 0)

        @pl.when(qi >= 1)
        def _():
            step(qi - 1, bias_ref[0, 0], None)

        step(qi, bias_ref[0, 1], None)

        @pl.when(qi + 1 < n_tiles)
        def _():
            step(qi + 1, bias_ref[0, 2], None)

        lax.fori_loop(qi + 2, n_tiles, far_right, 0)

        inv_l = 1.0 / l_ref[...]
        acc = acc_ref[...] * inv_l
        o = acc[:, :tile] - lam * acc[:, tile:]
        ms = jnp.mean(o * o, axis=0, keepdims=True)
        o = o * lax.rsqrt(ms + EPS) * subln_ref[...] * out_scale
        o_ref[pl.ds(q0, tile), :] = o.T.astype(BF16)
        return carry

    lax.fori_loop(0, n_tiles, q_body, 0)


def _attn_a(qt, k, vt, bias_tiles, far_const, lam, subln, batch, seq, out_scale):
    d = k.shape[1]
    hw = 2 * HEAD_DIM
    n_heads = d // hw
    tile = ATTN_A_TILE
    smem = pl.BlockSpec(memory_space=pltpu.SMEM)
    return pl.pallas_call(
        functools.partial(_attn_a_kernel, seq=seq, tile=tile, out_scale=out_scale),
        grid=(n_heads, batch),
        in_specs=[smem, smem,
                  pl.BlockSpec((hw, seq), lambda h, b: (h, b)),
                  pl.BlockSpec((seq, hw), lambda h, b: (b, h)),
                  pl.BlockSpec((hw, seq), lambda h, b: (h, b)),
                  pl.BlockSpec((1, 3, tile, tile), lambda h, b: (h, 0, 0, 0)),
                  _const_spec((hw, 1))],
        out_specs=pl.BlockSpec((seq, hw), lambda h, b: (b, h)),
        out_shape=jax.ShapeDtypeStruct(k.shape, BF16),
        scratch_shapes=[pltpu.VMEM((1, 2 * tile), F32), pltpu.VMEM((1, 2 * tile), F32),
                        pltpu.VMEM((hw, 2 * tile), F32)],
        compiler_params=_params("parallel", "parallel"),
        name="attn_diff",
    )(lam, far_const, qt, k, vt, bias_tiles, subln)


def _t5_bias_vec(t5_table, seq):
    nb = T5_BUCKETS // 2
    max_exact = nb // 2
    rel = jnp.arange(-(seq - 1), seq, dtype=jnp.int32)
    ret = jnp.where(rel > 0, nb, 0)
    n = jnp.abs(rel)
    n_f = jnp.maximum(n, 1).astype(F32)
    large = max_exact + (jnp.log(n_f / max_exact) / math.log(T5_MAX_DIST / max_exact)
                         * (nb - max_exact)).astype(jnp.int32)
    large = jnp.minimum(large, nb - 1)
    bucket = ret + jnp.where(n < max_exact, n, large)
    return t5_table.astype(F32)[bucket]


def _t5_tiles(t5_table, seq, tile):
    assert tile > T5_MAX_DIST and seq >= 2 * tile
    vec = _t5_bias_vec(t5_table, seq)
    kk = jnp.arange(tile, dtype=jnp.int32)[:, None]
    qq = jnp.arange(tile, dtype=jnp.int32)[None, :]
    idx = jnp.stack([(j * tile + kk - qq) + (seq - 1) for j in (-1, 0, 1)])
    tiles = jnp.transpose(vec[idx], (3, 0, 1, 2))
    far = jnp.stack([vec[0], vec[-1]], axis=1)
    return tiles, far


def _attn_b_kernel(qt_ref, k_ref, vt_ref, o_ref, m_ref, l_ref, acc_ref, *, seq, tq, tk, group):
    upper = pl.program_id(0) % 2 == 1

    def q_body(qi, carry):
        q0 = pl.multiple_of(qi * tq, tq)
        q_tile = qt_ref[:, pl.ds(q0, tq)]
        q_cat = jnp.concatenate([q_tile[r * HEAD_DIM:(r + 1) * HEAD_DIM] for r in range(group)],
                                axis=1)
        zeros = jnp.zeros_like(q_cat)
        q_pad = jnp.concatenate([jnp.where(upper, zeros, q_cat), jnp.where(upper, q_cat, zeros)],
                                axis=0)
        _reset(m_ref, l_ref, acc_ref)

        def k_body(ki, c):
            k0 = pl.multiple_of(ki * tk, tk)
            s = jnp.dot(k_ref[pl.ds(k0, tk), :], q_pad, preferred_element_type=F32)
            _softmax_step(s, None, vt_ref[:, pl.ds(k0, tk)], m_ref, l_ref, acc_ref)
            return c

        lax.fori_loop(0, seq // tk, k_body, 0)
        o = acc_ref[...] * (1.0 / l_ref[...])
        o_cat = jnp.concatenate([o[:, r * tq:(r + 1) * tq] for r in range(group)], axis=0)
        o_ref[pl.ds(q0, tq), :] = o_cat.T.astype(BF16)
        return carry

    lax.fori_loop(0, seq // tq, q_body, 0)


def _attn_b(qt, k, vt, batch, seq):
    dq = qt.shape[0]
    n_kv = vt.shape[0] // HEAD_DIM
    group = dq // HEAD_DIM // n_kv
    gw = group * HEAD_DIM
    tq, tk = ATTN_B_TQ, ATTN_B_TK
    return pl.pallas_call(
        functools.partial(_attn_b_kernel, seq=seq, tq=tq, tk=tk, group=group),
        grid=(n_kv, batch),
        in_specs=[pl.BlockSpec((gw, seq), lambda g, b: (g, b)),
                  pl.BlockSpec((seq, 2 * HEAD_DIM), lambda g, b: (b, g // 2)),
                  pl.BlockSpec((HEAD_DIM, seq), lambda g, b: (g, b))],
        out_specs=pl.BlockSpec((seq, gw), lambda g, b: (b, g)),
        out_shape=jax.ShapeDtypeStruct((batch * seq, dq), BF16),
        scratch_shapes=[pltpu.VMEM((1, group * tq), F32), pltpu.VMEM((1, group * tq), F32),
                        pltpu.VMEM((HEAD_DIM, group * tq), F32)],
        compiler_params=_params("parallel", "parallel"),
        name="attn_gqa",
    )(qt, k, vt)


def _rope_tables(seq):
    n = HEAD_DIM // 4
    t = jnp.arange(seq, dtype=jnp.int32)
    freqs = ROPE_THETA ** (-jnp.arange(n, dtype=F32) / n)

    def half(pos):
        ang = pos.astype(F32)[None, :] * freqs[:, None]
        c, s = jnp.cos(ang), jnp.sin(ang)
        return jnp.concatenate([c, c], axis=0), jnp.concatenate([-s, s], axis=0)

    c_row, s_row = half(t // GRID_W)
    c_col, s_col = half(t % GRID_W)
    return jnp.concatenate([c_row, c_col], axis=0), jnp.concatenate([s_row, s_col], axis=0)


def _attn_c_kernel(qt_ref, k_ref, vt_ref, bm_ref, o_ref, *, rows):
    j = pl.program_id(2)
    n_keys = NA_K_ROWS * GRID_W
    w0 = jnp.clip(j * NA_Q_ROWS - NA_ROWS // 2, 0, rows - NA_K_ROWS)
    k0 = pl.multiple_of(w0 * GRID_W, NA_ROWS // 2 * GRID_W)
    k_win = k_ref[pl.ds(k0, n_keys), :]
    q_tile = qt_ref[...]
    zeros = jnp.zeros_like(q_tile)
    first_head = lax.broadcasted_iota(jnp.int32, q_tile.shape, 0) < HEAD_DIM
    outs = []
    for hh in range(2):
        q_pad = jnp.where(first_head, q_tile, zeros) if hh == 0 else jnp.where(first_head, zeros, q_tile)
        s = jnp.dot(k_win, q_pad, preferred_element_type=F32) + bm_ref[hh, 0]
        m = jnp.max(s, axis=0, keepdims=True)
        p = jnp.exp(s - m)
        l = jnp.sum(p, axis=0, keepdims=True)
        v_win = vt_ref[hh * HEAD_DIM:(hh + 1) * HEAD_DIM, pl.ds(k0, n_keys)]
        outs.append(jnp.dot(v_win, p.astype(BF16), preferred_element_type=F32) * (1.0 / l))
    o_ref[...] = jnp.concatenate(outs, axis=0).T.astype(BF16)


def _attn_c(qt, k, vt, biasmask, batch, seq):
    d = k.shape[1]
    hw = 2 * HEAD_DIM
    rows = seq // GRID_W
    assert rows >= NA_K_ROWS and rows % NA_Q_ROWS == 0
    n_blk = rows // NA_Q_ROWS
    n_q = NA_Q_ROWS * GRID_W
    n_keys = NA_K_ROWS * GRID_W

    def variant(j):
        return jnp.where(j == 0, 0, jnp.where(j == n_blk - 1, 2, 1))

    return pl.pallas_call(
        functools.partial(_attn_c_kernel, rows=rows),
        grid=(d // hw, batch, n_blk),
        in_specs=[pl.BlockSpec((hw, n_q), lambda h, b, j: (h, b * n_blk + j)),
                  pl.BlockSpec((seq, hw), lambda h, b, j: (b, h)),
                  pl.BlockSpec((hw, seq), lambda h, b, j: (h, b)),
                  pl.BlockSpec((2, 1, n_keys, n_q), lambda h, b, j: (h, variant(j), 0, 0))],
        out_specs=pl.BlockSpec((n_q, hw), lambda h, b, j: (b * n_blk + j, h)),
        out_shape=jax.ShapeDtypeStruct(k.shape, BF16),
        compiler_params=_params("parallel", "parallel", "arbitrary"),
        name="attn_nbr",
    )(qt, k, vt, biasmask)


def _na_biasmask(rpb, rows):
    kk = jnp.arange(NA_K_ROWS * GRID_W, dtype=jnp.int32)
    qq = jnp.arange(NA_Q_ROWS * GRID_W, dtype=jnp.int32)
    kc, qc = kk % GRID_W, qq % GRID_W
    c_start = jnp.clip(qc - NA_COLS // 2, 0, GRID_W - NA_COLS)
    col_ok = (kc[:, None] >= c_start[None, :]) & (kc[:, None] < c_start[None, :] + NA_COLS)
    c_idx = jnp.clip(kc[:, None] - qc[None, :] + NA_COLS - 1, 0, 2 * NA_COLS - 2)
    n_blk = rows // NA_Q_ROWS
    tiles = []
    for j in (0, 1, n_blk - 1):
        w0 = min(max(j * NA_Q_ROWS - NA_ROWS // 2, 0), rows - NA_K_ROWS)
        kr = w0 + kk // GRID_W
        qr = j * NA_Q_ROWS + qq // GRID_W
        r_start = jnp.clip(qr - NA_ROWS // 2, 0, rows - NA_ROWS)
        row_ok = (kr[:, None] >= r_start[None, :]) & (kr[:, None] < r_start[None, :] + NA_ROWS)
        r_idx = jnp.clip(kr[:, None] - qr[None, :] + NA_ROWS - 1, 0, 2 * NA_ROWS - 2)
        bias = rpb.astype(F32)[:, r_idx, c_idx]
        tiles.append(jnp.where((row_ok & col_ok)[None], bias, NEG_INF))
    return jnp.stack(tiles, axis=1)


def _trunk(x3, w):
    batch, seq, d = x3.shape
    x = x3.reshape(batch * seq, d)
    depth = w["ffn_w_in"].shape[0]
    for i in range(depth):
        j, kind = divmod(i, N_MIXERS)
        x = _ffn(x, w["ffn_norm"][i, 0], w["ffn_w_in"][i, 0], w["ffn_w_out"][i, 0], w["final_norm"], False)
        if kind == 0:
            lambda_init = 0.8 - 0.6 * math.exp(-0.3 * i)
            qt, k, vt = _proj(x, w["mix_norm"][i], *w["a_qkv"][j])
            tiles, far = _t5_tiles(w["t5_table"], seq, ATTN_A_TILE)
            lamf = w["a_lambda"][j].astype(F32)
            lam = (jnp.exp(jnp.sum(lamf[0] * lamf[1])) - jnp.exp(jnp.sum(lamf[2] * lamf[3]))
                   + lambda_init).reshape(1)
            o = _attn_a(qt, k, vt, tiles, far, lam, w["a_subln"][j], batch, seq, 1.0 - lambda_init)
            w_out = w["a_w_out"][j]
        elif kind == 1:
            cos_t, sin_t = _rope_tables(seq)
            qt, k, vt = _proj_b(x, w["mix_norm"][i], *w["b_qkv"][j], w["b_q_norm"][j], w["b_k_norm"][j],
                                cos_t, sin_t)
            o = _attn_b(qt, k, vt, batch, seq)
            w_out = w["b_w_out"][j]
        else:
            qt, k, vt = _proj(x, w["mix_norm"][i], *w["c_qkv"][j])
            o = _attn_c(qt, k, vt, _na_biasmask(w["c_rpb"][j], seq // GRID_W), batch, seq)
            w_out = w["c_w_out"][j]
        x = _out_proj(x, o, w_out)
        x = _ffn(x, w["ffn_norm"][i, 1], w["ffn_w_in"][i, 1], w["ffn_w_out"][i, 1], w["final_norm"],
                 i == depth - 1)
    return x.reshape(batch, seq, d)


def _split_qkv_t(w_qkv, n_q, n_k):
    wq = jnp.swapaxes(w_qkv[:, :, :n_q], 1, 2).astype(BF16)
    wk = w_qkv[:, :, n_q:n_q + n_k].astype(BF16)
    wv = jnp.swapaxes(w_qkv[:, :, n_q + n_k:], 1, 2).astype(BF16)
    return [(wq[j], wk[j], wv[j]) for j in range(w_qkv.shape[0])]


def kernel(x_prompt, x_sample, ffn_norm, ffn_w_in, ffn_w_out, mix_norm, a_w_qkv, a_lambda, a_subln, a_w_out,
           t5_table, b_w_qkv, b_q_norm, b_k_norm, b_w_out, c_w_qkv, c_rpb, c_w_out, final_norm):
    d = x_prompt.shape[-1]
    n_kv = (b_w_qkv.shape[-1] - d) // 2
    b_qkv = _split_qkv_t(b_w_qkv, d, n_kv)
    w = {
        "ffn_norm": ffn_norm.astype(F32)[:, :, None, :],
        "ffn_w_in": ffn_w_in.astype(BF16),
        "ffn_w_out": ffn_w_out.astype(BF16),
        "mix_norm": mix_norm.astype(F32)[:, None, :],
        "final_norm": final_norm.astype(F32)[None, :],
        "a_qkv": _split_qkv_t(a_w_qkv, d, d),
        "a_lambda": a_lambda,
        "a_subln": a_subln.astype(F32)[:, :, None],
        "a_w_out": a_w_out.astype(BF16),
        "t5_table": t5_table,
        "b_qkv": [(wq, jnp.swapaxes(wk, 0, 1), wv) for wq, wk, wv in b_qkv],
        "b_q_norm": b_q_norm.astype(F32)[:, :, None],
        "b_k_norm": b_k_norm.astype(F32)[:, :, None],
        "b_w_out": b_w_out.astype(BF16),
        "c_qkv": _split_qkv_t(c_w_qkv, d, d),
        "c_rpb": c_rpb,
        "c_w_out": c_w_out.astype(BF16),
    }
    return (_trunk(x_prompt, w), _trunk(x_sample, w))
```

```python
import functools
import math

import numpy as np
import jax
import jax.numpy as jnp
from jax import lax
from jax.experimental import pallas as pl
from jax.experimental.pallas import tpu as pltpu

F32 = jnp.float32
BF16 = jnp.bfloat16

EPS = 1e-6
HEAD_DIM = 64
GRID_W = 64
NEG_INF = -1e30
N_MIXERS = 3
T5_BUCKETS = 32
T5_MAX_DIST = 128
ROPE_THETA = 10000.0
NA_ROWS = 8
NA_COLS = 16
LOG2E = math.log2(math.e)
Q_SCALE = HEAD_DIM ** -0.5 * LOG2E

V7X_VMEM_BYTES = 64 * 1024 * 1024
VMEM_LIMIT_BYTES = V7X_VMEM_BYTES - 8 * 1024 * 1024
MXU_DIM = 256
BF16_SUBLANES = 16
ROW_TILE = 512
FFN_CHUNK = MXU_DIM
ATTN_A_TILE = 512
T5_WINDOW = 4
ATTN_B_TQ = 256
ATTN_B_TK = 512
NA_Q_ROWS = 8
NA_K_ROWS = 16


def _params(*sem):
    return pltpu.CompilerParams(dimension_semantics=sem, vmem_limit_bytes=VMEM_LIMIT_BYTES)


def _const_spec(shape):
    nd = len(shape)
    return pl.BlockSpec(shape, lambda *_: (0,) * nd, pipeline_mode=pl.Buffered(1))


def _rms(x, gain):
    ms = jnp.mean(x * x, axis=-1, keepdims=True)
    return x * lax.rsqrt(ms + EPS) * gain


def _ffn_kernel(x_ref, g_ref, wi_ref, wo_ref, fg_ref, o_ref, act_ref, *, d_ff, final):
    x = x_ref[...]
    xn = _rms(x, g_ref[...]).astype(BF16)
    for c in range(d_ff // FFN_CHUNK):
        lo = c * FFN_CHUNK
        gate = jnp.dot(xn, wi_ref[:, lo:lo + FFN_CHUNK], preferred_element_type=F32)
        up = jnp.dot(xn, wi_ref[:, d_ff + lo:d_ff + lo + FFN_CHUNK], preferred_element_type=F32)
        act_ref[:, lo:lo + FFN_CHUNK] = (gate * jax.nn.sigmoid(gate) * up).astype(BF16)
    y = x + 0.5 * jnp.dot(act_ref[...], wo_ref[...], preferred_element_type=F32)
    if final:
        y = _rms(y, fg_ref[...])
    o_ref[...] = y


def _ffn(x, gain, w_in, w_out, final_gain, final):
    t, d = x.shape
    d_ff = w_out.shape[0]
    row = pl.BlockSpec((ROW_TILE, d), lambda i: (i, 0))
    return pl.pallas_call(
        functools.partial(_ffn_kernel, d_ff=d_ff, final=final),
        grid=(t // ROW_TILE,),
        in_specs=[row, _const_spec((1, d)), _const_spec(w_in.shape), _const_spec(w_out.shape),
                  _const_spec((1, d))],
        out_specs=row,
        out_shape=jax.ShapeDtypeStruct((t, d), F32),
        scratch_shapes=[pltpu.VMEM((ROW_TILE, d_ff), BF16)],
        compiler_params=_params("parallel"),
        name="ffn",
    )(x, gain, w_in, w_out, final_gain)


def _dot_nt(a, b):
    return lax.dot_general(a, b, (((1,), (1,)), ((), ())), preferred_element_type=F32)


def _proj_kernel(x_ref, g_ref, wqt_ref, wk_ref, wvt_ref, qt_ref, k_ref, vt_ref):
    hn = _rms(x_ref[...], g_ref[...]).astype(BF16)
    qt_ref[...] = (_dot_nt(wqt_ref[...], hn) * Q_SCALE).astype(BF16)
    k_ref[...] = jnp.dot(hn, wk_ref[...], preferred_element_type=F32).astype(BF16)
    vt_ref[...] = _dot_nt(wvt_ref[...], hn).astype(BF16)


def _proj(x, gain, wqt, wk, wvt):
    t, d = x.shape
    row = pl.BlockSpec((ROW_TILE, d), lambda i: (i, 0))
    col = pl.BlockSpec((d, ROW_TILE), lambda i: (0, i))
    return pl.pallas_call(
        _proj_kernel,
        grid=(t // ROW_TILE,),
        in_specs=[row, _const_spec((1, d)), _const_spec(wqt.shape), _const_spec(wk.shape),
                  _const_spec(wvt.shape)],
        out_specs=[col, row, col],
        out_shape=[jax.ShapeDtypeStruct((d, t), BF16), jax.ShapeDtypeStruct((t, d), BF16),
                   jax.ShapeDtypeStruct((d, t), BF16)],
        compiler_params=_params("parallel"),
        name="proj_qkv",
    )(x, gain, wqt, wk, wvt)


def _norm_rope_t(yt, gain, cos, sin, n_heads):
    q4 = HEAD_DIM // 4
    out = []
    for h in range(n_heads):
        y = yt[h * HEAD_DIM:(h + 1) * HEAD_DIM]
        ms = jnp.mean(y * y, axis=0, keepdims=True)
        yn = y * lax.rsqrt(ms + EPS) * gain
        swapped = jnp.concatenate([yn[q4:2 * q4], yn[:q4], yn[3 * q4:], yn[2 * q4:3 * q4]], axis=0)
        out.append(yn * cos + swapped * sin)
    return jnp.concatenate(out, axis=0)


def _proj_b_kernel(x_ref, g_ref, wqt_ref, wkt_ref, wvt_ref, qn_ref, kn_ref, cos_ref, sin_ref,
                   qt_ref, k_ref, vt_ref, *, n_q, n_kv):
    hn = _rms(x_ref[...], g_ref[...]).astype(BF16)
    cos = cos_ref[...]
    sin = sin_ref[...]
    q = _norm_rope_t(_dot_nt(wqt_ref[...], hn), qn_ref[...], cos, sin, n_q)
    qt_ref[...] = (q * Q_SCALE).astype(BF16)
    k = _norm_rope_t(_dot_nt(wkt_ref[...], hn), kn_ref[...], cos, sin, n_kv)
    k_ref[...] = k.T.astype(BF16)
    vt_ref[...] = _dot_nt(wvt_ref[...], hn).astype(BF16)


def _proj_b(x, gain, wqt, wkt, wvt, q_norm, k_norm, cos_t, sin_t):
    t, d = x.shape
    dq, dkv = wqt.shape[0], wkt.shape[0]
    s_tiles = cos_t.shape[1] // ROW_TILE
    row = pl.BlockSpec((ROW_TILE, d), lambda i: (i, 0))
    tab = pl.BlockSpec((HEAD_DIM, ROW_TILE), lambda i: (0, i % s_tiles))
    return pl.pallas_call(
        functools.partial(_proj_b_kernel, n_q=dq // HEAD_DIM, n_kv=dkv // HEAD_DIM),
        grid=(t // ROW_TILE,),
        in_specs=[row, _const_spec((1, d)), _const_spec(wqt.shape), _const_spec(wkt.shape),
                  _const_spec(wvt.shape), _const_spec((HEAD_DIM, 1)), _const_spec((HEAD_DIM, 1)),
                  tab, tab],
        out_specs=[pl.BlockSpec((dq, ROW_TILE), lambda i: (0, i)),
                   pl.BlockSpec((ROW_TILE, dkv), lambda i: (i, 0)),
                   pl.BlockSpec((dkv, ROW_TILE), lambda i: (0, i))],
        out_shape=[jax.ShapeDtypeStruct((dq, t), BF16), jax.ShapeDtypeStruct((t, dkv), BF16),
                   jax.ShapeDtypeStruct((dkv, t), BF16)],
        compiler_params=_params("parallel"),
        name="proj_qkv_rope",
    )(x, gain, wqt, wkt, wvt, q_norm, k_norm, cos_t, sin_t)


def _out_proj_kernel(x_ref, o_ref, w_ref, y_ref):
    y_ref[...] = x_ref[...] + jnp.dot(o_ref[...], w_ref[...], preferred_element_type=F32)


def _out_proj(x, o, w):
    t, d = x.shape
    row = pl.BlockSpec((ROW_TILE, d), lambda i: (i, 0))
    return pl.pallas_call(
        _out_proj_kernel,
        grid=(t // ROW_TILE,),
        in_specs=[row, row, _const_spec(w.shape)],
        out_specs=row,
        out_shape=jax.ShapeDtypeStruct((t, d), F32),
        compiler_params=_params("parallel"),
        name="out_proj",
    )(x, o, w)


class _Pipe:
    def __init__(self, k_ref, vt_ref, qm_ref, s0, s1, c0, c1, p0, p1, a0, a1, m_ref, acc_ref, tk):
        self.k_ref, self.vt_ref, self.qm_ref = k_ref, vt_ref, qm_ref
        self.s, self.c, self.p, self.a = (s0, s1), (c0, c1), (p0, p1), (a0, a1)
        self.m_ref, self.acc_ref, self.tk = m_ref, acc_ref, tk

    def reset(self):
        self.m_ref[...] = jnp.full(self.m_ref.shape, NEG_INF, F32)
        self.acc_ref[...] = jnp.zeros(self.acc_ref.shape, F32)

    def scores(self, slot, tile, bias=None):
        k0 = pl.multiple_of(tile * self.tk, self.tk)
        s = jnp.dot(self.k_ref[pl.ds(k0, self.tk), :], self.qm_ref[...], preferred_element_type=F32)
        if bias is not None:
            w = bias.shape[1]
            s = jnp.concatenate([s[:, j:j + w] + bias for j in range(0, s.shape[1], w)], axis=1)
        self.s[slot][...] = s
        self.c[slot][...] = jnp.max(s, axis=0, keepdims=True)

    def softmax(self, slot, shift_const=None):
        s = self.s[slot][...]
        m_prev = self.m_ref[...]
        s_max = self.c[slot][...]
        if shift_const is not None:
            m_new = jnp.maximum(m_prev, s_max + shift_const)
            shift = m_new - shift_const
        else:
            m_new = jnp.maximum(m_prev, s_max)
            shift = m_new
        self.p[slot][...] = jnp.exp2(s - shift).astype(BF16)
        self.a[slot][...] = jnp.exp2(m_prev - m_new)
        self.m_ref[...] = m_new

    def values(self, slot, tile):
        k0 = pl.multiple_of(tile * self.tk, self.tk)
        ones = jnp.ones((BF16_SUBLANES, self.tk), BF16)
        vt = jnp.concatenate([self.vt_ref[:, pl.ds(k0, self.tk)], ones], axis=0)
        self.acc_ref[...] = (self.a[slot][...] * self.acc_ref[...]
                             + jnp.dot(vt, self.p[slot][...], preferred_element_type=F32))


def _pipe_scratch(tk, n_cols, dv):
    return [pltpu.VMEM((2 * HEAD_DIM, n_cols), BF16),
            pltpu.VMEM((tk, n_cols), F32), pltpu.VMEM((tk, n_cols), F32),
            pltpu.VMEM((1, n_cols), F32), pltpu.VMEM((1, n_cols), F32),
            pltpu.VMEM((tk, n_cols), BF16), pltpu.VMEM((tk, n_cols), BF16),
            pltpu.VMEM((1, n_cols), F32), pltpu.VMEM((1, n_cols), F32),
            pltpu.VMEM((1, n_cols), F32),
            pltpu.VMEM((dv + BF16_SUBLANES, n_cols), F32)]


def _attn_a_kernel(lam_ref, far_ref, qt_ref, k_ref, vt_ref, bias_ref, subln_ref, o_ref,
                   qm_ref, s0, s1, c0, c1, p0, p1, a0, a1, m_ref, acc_ref, *, seq, tile, out_scale):
    head = pl.program_id(0)
    n_tiles = seq // tile
    n_far = n_tiles - T5_WINDOW
    dv = 2 * HEAD_DIM
    lam = lam_ref[0]
    c_left = far_ref[head, 0]
    c_right = far_ref[head, 1]
    first_half = lax.broadcasted_iota(jnp.int32, (dv, tile), 0) < HEAD_DIM
    pipe = _Pipe(k_ref, vt_ref, qm_ref, s0, s1, c0, c1, p0, p1, a0, a1, m_ref, acc_ref, tile)

    def q_body(qi, carry):
        q0 = pl.multiple_of(qi * tile, tile)
        q_tile = qt_ref[:, pl.ds(q0, tile)]
        zeros = jnp.zeros_like(q_tile)
        qm_ref[...] = jnp.concatenate([jnp.where(first_half, q_tile, zeros),
                                       jnp.where(first_half, zeros, q_tile)], axis=1)
        pipe.reset()
        w0 = jnp.clip(qi - 1, 0, n_far)

        def far_tile(f):
            f = jnp.minimum(f, n_far - 1)
            return jnp.where(f < w0, f, f + T5_WINDOW)

        def window_scores(i):
            pipe.scores(i % 2, w0 + i, bias=bias_ref[0, w0 + i - qi + (T5_WINDOW - 1)])

        window_scores(0)
        for i in range(T5_WINDOW):
            if i > 0:
                pipe.values((i - 1) % 2, w0 + i - 1)
            pipe.softmax(i % 2)
            if i + 1 < T5_WINDOW:
                window_scores(i + 1)
            elif n_far > 0:
                pipe.scores((i + 1) % 2, far_tile(0))

        def far_pair(g, c):
            for u in range(2):
                f = 2 * g + u
                prev = jnp.where(f == 0, w0 + T5_WINDOW - 1, far_tile(f - 1))
                pipe.values(1 - u, prev)
                pipe.softmax(u, shift_const=jnp.where(f < w0, c_left, c_right))
                pipe.scores(1 - u, far_tile(f + 1))
            return c

        if n_far > 0:
            lax.fori_loop(0, n_far // 2, far_pair, 0)
            pipe.values(1, far_tile(n_far - 1))
        else:
            pipe.values(1, w0 + T5_WINDOW - 1)

        acc = acc_ref[...]
        acc = acc[:dv] * (1.0 / acc[dv:dv + 1])
        o = acc[:, :tile] - lam * acc[:, tile:]
        ms = jnp.mean(o * o, axis=0, keepdims=True)
        o = o * lax.rsqrt(ms + EPS) * subln_ref[...] * out_scale
        o_ref[pl.ds(q0, tile), :] = o.T.astype(BF16)
        return carry

    lax.fori_loop(0, n_tiles, q_body, 0)


def _attn_a(qt, k, vt, bias_tiles, far_const, lam, subln, batch, seq, out_scale):
    d = k.shape[1]
    hw = 2 * HEAD_DIM
    tile = bias_tiles.shape[-1]
    n_tiles = seq // tile
    assert n_tiles >= T5_WINDOW and n_tiles % 2 == 0
    smem = pl.BlockSpec(memory_space=pltpu.SMEM)
    return pl.pallas_call(
        functools.partial(_attn_a_kernel, seq=seq, tile=tile, out_scale=out_scale),
        grid=(d // hw, batch),
        in_specs=[smem, smem,
                  pl.BlockSpec((hw, seq), lambda h, b: (h, b)),
                  pl.BlockSpec((seq, hw), lambda h, b: (b, h)),
                  pl.BlockSpec((hw, seq), lambda h, b: (h, b)),
                  pl.BlockSpec((1, 2 * T5_WINDOW - 1, tile, tile), lambda h, b: (h, 0, 0, 0),
                               pipeline_mode=pl.Buffered(1)),
                  _const_spec((hw, 1))],
        out_specs=pl.BlockSpec((seq, hw), lambda h, b: (b, h)),
        out_shape=jax.ShapeDtypeStruct(k.shape, BF16),
        scratch_shapes=_pipe_scratch(tile, 2 * tile, hw),
        compiler_params=_params("parallel", "parallel"),
        name="attn_diff",
    )(lam, far_const, qt, k, vt, bias_tiles, subln)


def _t5_bias_vec(t5_table, span):
    nb = T5_BUCKETS // 2
    max_exact = nb // 2
    rel = jnp.arange(-(span - 1), span, dtype=jnp.int32)
    ret = jnp.where(rel > 0, nb, 0)
    n = jnp.abs(rel)
    n_f = jnp.maximum(n, 1).astype(F32)
    large = max_exact + (jnp.log(n_f / max_exact) / math.log(T5_MAX_DIST / max_exact)
                         * (nb - max_exact)).astype(jnp.int32)
    large = jnp.minimum(large, nb - 1)
    bucket = ret + jnp.where(n < max_exact, n, large)
    return t5_table.astype(F32)[bucket]


def _t5_tiles(t5_table, tile):
    assert tile > T5_MAX_DIST
    span = T5_WINDOW * tile
    vec_t = _t5_bias_vec(t5_table, span + 1).T * LOG2E
    period = 2 * tile
    y = np.arange(period)
    shift = np.where(y <= tile, -y, period - y)
    rel = np.stack([j * tile + shift for j in range(1 - T5_WINDOW, T5_WINDOW)])
    gen = vec_t[:, rel + span]
    flat = jnp.tile(gen, (1, 1, tile))[:, :, :tile * (period - 1)]
    tiles = flat.reshape(gen.shape[0], gen.shape[1], tile, period - 1)[:, :, :, :tile]
    far = jnp.stack([vec_t[:, 0], vec_t[:, -1]], axis=1)
    return tiles, far


def _attn_b_kernel(qt_ref, k_ref, vt_ref, o_ref, qm_ref, s0, s1, c0, c1, p0, p1, a0, a1, m_ref, acc_ref,
                   *, seq, tq, tk, group):
    upper = pl.program_id(0) % 2 == 1
    n_tiles = seq // tk
    pipe = _Pipe(k_ref, vt_ref, qm_ref, s0, s1, c0, c1, p0, p1, a0, a1, m_ref, acc_ref, tk)

    def q_body(qi, carry):
        q0 = pl.multiple_of(qi * tq, tq)
        q_tile = qt_ref[:, pl.ds(q0, tq)]
        q_cat = jnp.concatenate([q_tile[r * HEAD_DIM:(r + 1) * HEAD_DIM] for r in range(group)],
                                axis=1)
        zeros = jnp.zeros_like(q_cat)
        qm_ref[...] = jnp.concatenate([jnp.where(upper, zeros, q_cat),
                                       jnp.where(upper, q_cat, zeros)], axis=0)
        pipe.reset()
        pipe.scores(0, 0)
        pipe.softmax(0)
        pipe.scores(1, 1)
        pipe.values(0, 0)
        pipe.softmax(1)
        pipe.scores(0, min(2, n_tiles - 1))

        def pair(g, c):
            for u in range(2):
                t = 2 * g + u
                pipe.values(1 - u, t - 1)
                pipe.softmax(u)
                pipe.scores(1 - u, jnp.minimum(t + 1, n_tiles - 1))
            return c

        lax.fori_loop(1, n_tiles // 2, pair, 0)
        pipe.values(1, n_tiles - 1)
        acc = acc_ref[...]
        o = acc[:HEAD_DIM] * (1.0 / acc[HEAD_DIM:HEAD_DIM + 1])
        o_cat = jnp.concatenate([o[:, r * tq:(r + 1) * tq] for r in range(group)], axis=0)
        o_ref[pl.ds(q0, tq), :] = o_cat.T.astype(BF16)
        return carry

    lax.fori_loop(0, seq // tq, q_body, 0)


def _attn_b(qt, k, vt, batch, seq):
    dq = qt.shape[0]
    n_kv = vt.shape[0] // HEAD_DIM
    group = dq // HEAD_DIM // n_kv
    gw = group * HEAD_DIM
    tq, tk = ATTN_B_TQ, ATTN_B_TK
    assert seq % (2 * tk) == 0 and seq % tq == 0
    return pl.pallas_call(
        functools.partial(_attn_b_kernel, seq=seq, tq=tq, tk=tk, group=group),
        grid=(n_kv, batch),
        in_specs=[pl.BlockSpec((gw, seq), lambda g, b: (g, b)),
                  pl.BlockSpec((seq, 2 * HEAD_DIM), lambda g, b: (b, g // 2)),
                  pl.BlockSpec((HEAD_DIM, seq), lambda g, b: (g, b))],
        out_specs=pl.BlockSpec((seq, gw), lambda g, b: (b, g)),
        out_shape=jax.ShapeDtypeStruct((batch * seq, dq), BF16),
        scratch_shapes=_pipe_scratch(tk, group * tq, HEAD_DIM),
        compiler_params=_params("parallel", "parallel"),
        name="attn_gqa",
    )(qt, k, vt)


def _rope_tables(seq):
    n = HEAD_DIM // 4
    t = jnp.arange(seq, dtype=jnp.int32)
    freqs = ROPE_THETA ** (-jnp.arange(n, dtype=F32) / n)

    def half(pos):
        ang = pos.astype(F32)[None, :] * freqs[:, None]
        c, s = jnp.cos(ang), jnp.sin(ang)
        return jnp.concatenate([c, c], axis=0), jnp.concatenate([-s, s], axis=0)

    c_row, s_row = half(t // GRID_W)
    c_col, s_col = half(t % GRID_W)
    return jnp.concatenate([c_row, c_col], axis=0), jnp.concatenate([s_row, s_col], axis=0)


def _attn_c_kernel(qt_ref, k_ref, vt_ref, bm_ref, o_ref, *, rows):
    j = pl.program_id(2)
    n_keys = NA_K_ROWS * GRID_W
    w0 = jnp.clip(j * NA_Q_ROWS - NA_ROWS // 2, 0, rows - NA_K_ROWS)
    k0 = pl.multiple_of(w0 * GRID_W, NA_ROWS // 2 * GRID_W)
    k_win = k_ref[pl.ds(k0, n_keys), :]
    q_tile = qt_ref[...]
    zeros = jnp.zeros_like(q_tile)
    first_head = lax.broadcasted_iota(jnp.int32, q_tile.shape, 0) < HEAD_DIM
    ones = jnp.ones((BF16_SUBLANES, n_keys), BF16)
    outs = []
    for hh in range(2):
        q_pad = jnp.where(first_head, q_tile, zeros) if hh == 0 else jnp.where(first_head, zeros, q_tile)
        s = jnp.dot(k_win, q_pad, preferred_element_type=F32) + bm_ref[hh, 0]
        p = jnp.exp2(s - jnp.max(s, axis=0, keepdims=True)).astype(BF16)
        v_win = jnp.concatenate([vt_ref[hh * HEAD_DIM:(hh + 1) * HEAD_DIM, pl.ds(k0, n_keys)], ones],
                                axis=0)
        acc = jnp.dot(v_win, p, preferred_element_type=F32)
        outs.append(acc[:HEAD_DIM] * (1.0 / acc[HEAD_DIM:HEAD_DIM + 1]))
    o_ref[...] = jnp.concatenate(outs, axis=0).T.astype(BF16)


def _attn_c(qt, k, vt, biasmask, batch, seq):
    d = k.shape[1]
    hw = 2 * HEAD_DIM
    rows = seq // GRID_W
    assert rows >= NA_K_ROWS and rows % NA_Q_ROWS == 0
    n_blk = rows // NA_Q_ROWS
    n_q = NA_Q_ROWS * GRID_W
    n_keys = NA_K_ROWS * GRID_W

    def variant(j):
        return jnp.where(j == 0, 0, jnp.where(j == n_blk - 1, 2, 1))

    return pl.pallas_call(
        functools.partial(_attn_c_kernel, rows=rows),
        grid=(d // hw, batch, n_blk),
        in_specs=[pl.BlockSpec((hw, n_q), lambda h, b, j: (h, b * n_blk + j)),
                  pl.BlockSpec((seq, hw), lambda h, b, j: (b, h)),
                  pl.BlockSpec((hw, seq), lambda h, b, j: (h, b)),
                  pl.BlockSpec((2, 1, n_keys, n_q), lambda h, b, j: (h, variant(j), 0, 0))],
        out_specs=pl.BlockSpec((n_q, hw), lambda h, b, j: (b * n_blk + j, h)),
        out_shape=jax.ShapeDtypeStruct(k.shape, BF16),
        compiler_params=_params("parallel", "parallel", "arbitrary"),
        name="attn_nbr",
    )(qt, k, vt, biasmask)


def _na_biasmask(rpb):
    rows = 4 * NA_Q_ROWS
    kk = np.arange(NA_K_ROWS * GRID_W)
    qq = np.arange(NA_Q_ROWS * GRID_W)
    kc, qc = kk % GRID_W, qq % GRID_W
    c_start = np.clip(qc - NA_COLS // 2, 0, GRID_W - NA_COLS)
    col_ok = (kc[:, None] >= c_start[None, :]) & (kc[:, None] < c_start[None, :] + NA_COLS)
    c_idx = np.clip(kc[:, None] - qc[None, :] + NA_COLS - 1, 0, 2 * NA_COLS - 2)
    n_blk = rows // NA_Q_ROWS
    rpb_l2 = rpb.astype(F32) * LOG2E
    tiles = []
    for j in (0, 1, n_blk - 1):
        w0 = min(max(j * NA_Q_ROWS - NA_ROWS // 2, 0), rows - NA_K_ROWS)
        kr = w0 + kk // GRID_W
        qr = j * NA_Q_ROWS + qq // GRID_W
        r_start = np.clip(qr - NA_ROWS // 2, 0, rows - NA_ROWS)
        row_ok = (kr[:, None] >= r_start[None, :]) & (kr[:, None] < r_start[None, :] + NA_ROWS)
        r_idx = np.clip(kr[:, None] - qr[None, :] + NA_ROWS - 1, 0, 2 * NA_ROWS - 2)
        bias = rpb_l2[:, r_idx, c_idx]
        tiles.append(jnp.where((row_ok & col_ok)[None], bias, NEG_INF))
    return jnp.stack(tiles, axis=1)


def _trunk(x3, w):
    batch, seq, d = x3.shape
    x = x3.reshape(batch * seq, d)
    depth = w["ffn_w_in"].shape[0]
    for i in range(depth):
        j, kind = divmod(i, N_MIXERS)
        x = _ffn(x, w["ffn_norm"][i, 0], w["ffn_w_in"][i, 0], w["ffn_w_out"][i, 0], w["final_norm"], False)
        if kind == 0:
            lambda_init = 0.8 - 0.6 * math.exp(-0.3 * i)
            qt, k, vt = _proj(x, w["mix_norm"][i], *w["a_qkv"][j])
            lamf = w["a_lambda"][j].astype(F32)
            lam = (jnp.exp(jnp.sum(lamf[0] * lamf[1])) - jnp.exp(jnp.sum(lamf[2] * lamf[3]))
                   + lambda_init).reshape(1)
            o = _attn_a(qt, k, vt, *w["t5"], lam, w["a_subln"][j], batch, seq, 1.0 - lambda_init)
            w_out = w["a_w_out"][j]
        elif kind == 1:
            cos_t, sin_t = _rope_tables(seq)
            qt, k, vt = _proj_b(x, w["mix_norm"][i], *w["b_qkv"][j], w["b_q_norm"][j], w["b_k_norm"][j],
                                cos_t, sin_t)
            o = _attn_b(qt, k, vt, batch, seq)
            w_out = w["b_w_out"][j]
        else:
            qt, k, vt = _proj(x, w["mix_norm"][i], *w["c_qkv"][j])
            o = _attn_c(qt, k, vt, w["c_biasmask"][j], batch, seq)
            w_out = w["c_w_out"][j]
        x = _out_proj(x, o, w_out)
        x = _ffn(x, w["ffn_norm"][i, 1], w["ffn_w_in"][i, 1], w["ffn_w_out"][i, 1], w["final_norm"],
                 i == depth - 1)
    return x.reshape(batch, seq, d)


def _split_qkv_t(w_qkv, n_q, n_k):
    wq = jnp.swapaxes(w_qkv[:, :, :n_q], 1, 2).astype(BF16)
    wk = w_qkv[:, :, n_q:n_q + n_k].astype(BF16)
    wv = jnp.swapaxes(w_qkv[:, :, n_q + n_k:], 1, 2).astype(BF16)
    return [(wq[j], wk[j], wv[j]) for j in range(w_qkv.shape[0])]


def kernel(x_prompt, x_sample, ffn_norm, ffn_w_in, ffn_w_out, mix_norm, a_w_qkv, a_lambda, a_subln, a_w_out,
           t5_table, b_w_qkv, b_q_norm, b_k_norm, b_w_out, c_w_qkv, c_rpb, c_w_out, final_norm):
    d = x_prompt.shape[-1]
    n_kv = (b_w_qkv.shape[-1] - d) // 2
    b_qkv = _split_qkv_t(b_w_qkv, d, n_kv)
    w = {
        "ffn_norm": ffn_norm.astype(F32)[:, :, None, :],
        "ffn_w_in": ffn_w_in.astype(BF16),
        "ffn_w_out": ffn_w_out.astype(BF16),
        "mix_norm": mix_norm.astype(F32)[:, None, :],
        "final_norm": final_norm.astype(F32)[None, :],
        "a_qkv": _split_qkv_t(a_w_qkv, d, d),
        "a_lambda": a_lambda,
        "a_subln": a_subln.astype(F32)[:, :, None],
        "a_w_out": a_w_out.astype(BF16),
        "t5": _t5_tiles(t5_table, ATTN_A_TILE),
        "b_qkv": [(wq, jnp.swapaxes(wk, 0, 1), wv) for wq, wk, wv in b_qkv],
        "b_q_norm": b_q_norm.astype(F32)[:, :, None],
        "b_k_norm": b_k_norm.astype(F32)[:, :, None],
        "b_w_out": b_w_out.astype(BF16),
        "c_qkv": _split_qkv_t(c_w_qkv, d, d),
        "c_biasmask": [_na_biasmask(c_rpb[j]) for j in range(c_rpb.shape[0])],
        "c_w_out": c_w_out.astype(BF16),
    }
    return (_trunk(x_prompt, w), _trunk(x_sample, w))
```

```python
import functools
import math

import numpy as np
import jax
import jax.numpy as jnp
from jax import lax
from jax.experimental import pallas as pl
from jax.experimental.pallas import tpu as pltpu

F32 = jnp.float32
BF16 = jnp.bfloat16

EPS = 1e-6
HEAD_DIM = 64
GRID_W = 64
NEG_INF = -1e30
N_MIXERS = 3
T5_BUCKETS = 32
T5_MAX_DIST = 128
ROPE_THETA = 10000.0
NA_ROWS = 8
NA_COLS = 16
LOG2E = math.log2(math.e)
Q_SCALE = HEAD_DIM ** -0.5 * LOG2E

V7X_VMEM_BYTES = 64 * 1024 * 1024
VMEM_LIMIT_BYTES = V7X_VMEM_BYTES - 8 * 1024 * 1024
MXU_DIM = 256
BF16_SUBLANES = 16
ROW_TILE = 512
FFN_CHUNK = MXU_DIM
ATTN_A_TILE = 512
PIPE_CHUNK = 256
T5_WINDOW = 4
ATTN_B_TQ = 256
ATTN_B_TK = 512
NA_Q_ROWS = 8
NA_K_ROWS = 16


def _params(*sem):
    return pltpu.CompilerParams(dimension_semantics=sem, vmem_limit_bytes=VMEM_LIMIT_BYTES)


def _const_spec(shape):
    nd = len(shape)
    return pl.BlockSpec(shape, lambda *_: (0,) * nd, pipeline_mode=pl.Buffered(1))


def _rms(x, gain):
    ms = jnp.mean(x * x, axis=-1, keepdims=True)
    return x * lax.rsqrt(ms + EPS) * gain


def _ffn_kernel(x_ref, g_ref, wi_ref, wo_ref, fg_ref, o_ref, act_ref, *, d_ff, final):
    x = x_ref[...]
    xn = _rms(x, g_ref[...]).astype(BF16)
    for c in range(d_ff // FFN_CHUNK):
        lo = c * FFN_CHUNK
        gate = jnp.dot(xn, wi_ref[:, lo:lo + FFN_CHUNK], preferred_element_type=F32)
        up = jnp.dot(xn, wi_ref[:, d_ff + lo:d_ff + lo + FFN_CHUNK], preferred_element_type=F32)
        act_ref[:, lo:lo + FFN_CHUNK] = (gate * jax.nn.sigmoid(gate) * up).astype(BF16)
    y = x + 0.5 * jnp.dot(act_ref[...], wo_ref[...], preferred_element_type=F32)
    if final:
        y = _rms(y, fg_ref[...])
    o_ref[...] = y


def _ffn(x, gain, w_in, w_out, final_gain, final):
    t, d = x.shape
    d_ff = w_out.shape[0]
    row = pl.BlockSpec((ROW_TILE, d), lambda i: (i, 0))
    return pl.pallas_call(
        functools.partial(_ffn_kernel, d_ff=d_ff, final=final),
        grid=(t // ROW_TILE,),
        in_specs=[row, _const_spec((1, d)), _const_spec(w_in.shape), _const_spec(w_out.shape),
                  _const_spec((1, d))],
        out_specs=row,
        out_shape=jax.ShapeDtypeStruct((t, d), F32),
        scratch_shapes=[pltpu.VMEM((ROW_TILE, d_ff), BF16)],
        compiler_params=_params("parallel"),
        name="ffn",
    )(x, gain, w_in, w_out, final_gain)


def _dot_nt(a, b):
    return lax.dot_general(a, b, (((1,), (1,)), ((), ())), preferred_element_type=F32)


def _proj_kernel(x_ref, g_ref, wqt_ref, wk_ref, wvt_ref, qt_ref, k_ref, vt_ref):
    hn = _rms(x_ref[...], g_ref[...]).astype(BF16)
    qt_ref[...] = (_dot_nt(wqt_ref[...], hn) * Q_SCALE).astype(BF16)
    k_ref[...] = jnp.dot(hn, wk_ref[...], preferred_element_type=F32).astype(BF16)
    vt_ref[...] = _dot_nt(wvt_ref[...], hn).astype(BF16)


def _proj(x, gain, wqt, wk, wvt):
    t, d = x.shape
    row = pl.BlockSpec((ROW_TILE, d), lambda i: (i, 0))
    col = pl.BlockSpec((d, ROW_TILE), lambda i: (0, i))
    return pl.pallas_call(
        _proj_kernel,
        grid=(t // ROW_TILE,),
        in_specs=[row, _const_spec((1, d)), _const_spec(wqt.shape), _const_spec(wk.shape),
                  _const_spec(wvt.shape)],
        out_specs=[col, row, col],
        out_shape=[jax.ShapeDtypeStruct((d, t), BF16), jax.ShapeDtypeStruct((t, d), BF16),
                   jax.ShapeDtypeStruct((d, t), BF16)],
        compiler_params=_params("parallel"),
        name="proj_qkv",
    )(x, gain, wqt, wk, wvt)


def _norm_rope_t(yt, gain, cos, sin, n_heads):
    q4 = HEAD_DIM // 4
    out = []
    for h in range(n_heads):
        y = yt[h * HEAD_DIM:(h + 1) * HEAD_DIM]
        ms = jnp.mean(y * y, axis=0, keepdims=True)
        yn = y * lax.rsqrt(ms + EPS) * gain
        swapped = jnp.concatenate([yn[q4:2 * q4], yn[:q4], yn[3 * q4:], yn[2 * q4:3 * q4]], axis=0)
        out.append(yn * cos + swapped * sin)
    return jnp.concatenate(out, axis=0)


def _proj_b_kernel(x_ref, g_ref, wqt_ref, wkt_ref, wvt_ref, qn_ref, kn_ref, cos_ref, sin_ref,
                   qt_ref, k_ref, vt_ref, *, n_q, n_kv):
    hn = _rms(x_ref[...], g_ref[...]).astype(BF16)
    cos = cos_ref[...]
    sin = sin_ref[...]
    q = _norm_rope_t(_dot_nt(wqt_ref[...], hn), qn_ref[...], cos, sin, n_q)
    qt_ref[...] = (q * Q_SCALE).astype(BF16)
    k = _norm_rope_t(_dot_nt(wkt_ref[...], hn), kn_ref[...], cos, sin, n_kv)
    k_ref[...] = k.T.astype(BF16)
    vt_ref[...] = _dot_nt(wvt_ref[...], hn).astype(BF16)


def _proj_b(x, gain, wqt, wkt, wvt, q_norm, k_norm, cos_t, sin_t):
    t, d = x.shape
    dq, dkv = wqt.shape[0], wkt.shape[0]
    s_tiles = cos_t.shape[1] // ROW_TILE
    row = pl.BlockSpec((ROW_TILE, d), lambda i: (i, 0))
    tab = pl.BlockSpec((HEAD_DIM, ROW_TILE), lambda i: (0, i % s_tiles))
    return pl.pallas_call(
        functools.partial(_proj_b_kernel, n_q=dq // HEAD_DIM, n_kv=dkv // HEAD_DIM),
        grid=(t // ROW_TILE,),
        in_specs=[row, _const_spec((1, d)), _const_spec(wqt.shape), _const_spec(wkt.shape),
                  _const_spec(wvt.shape), _const_spec((HEAD_DIM, 1)), _const_spec((HEAD_DIM, 1)),
                  tab, tab],
        out_specs=[pl.BlockSpec((dq, ROW_TILE), lambda i: (0, i)),
                   pl.BlockSpec((ROW_TILE, dkv), lambda i: (i, 0)),
                   pl.BlockSpec((dkv, ROW_TILE), lambda i: (0, i))],
        out_shape=[jax.ShapeDtypeStruct((dq, t), BF16), jax.ShapeDtypeStruct((t, dkv), BF16),
                   jax.ShapeDtypeStruct((dkv, t), BF16)],
        compiler_params=_params("parallel"),
        name="proj_qkv_rope",
    )(x, gain, wqt, wkt, wvt, q_norm, k_norm, cos_t, sin_t)


def _out_proj_kernel(x_ref, o_ref, w_ref, y_ref):
    y_ref[...] = x_ref[...] + jnp.dot(o_ref[...], w_ref[...], preferred_element_type=F32)


def _out_proj(x, o, w):
    t, d = x.shape
    row = pl.BlockSpec((ROW_TILE, d), lambda i: (i, 0))
    return pl.pallas_call(
        _out_proj_kernel,
        grid=(t // ROW_TILE,),
        in_specs=[row, row, _const_spec(w.shape)],
        out_specs=row,
        out_shape=jax.ShapeDtypeStruct((t, d), F32),
        compiler_params=_params("parallel"),
        name="out_proj",
    )(x, o, w)


class _Pipe:
    def __init__(self, k_ref, vt_ref, qm_ref, s0, s1, c0, c1, p0, p1, a0, a1, m_ref, acc_ref, tk):
        self.k_ref, self.vt_ref, self.qm_ref = k_ref, vt_ref, qm_ref
        self.s, self.c, self.p, self.a = (s0, s1), (c0, c1), (p0, p1), (a0, a1)
        self.m_ref, self.acc_ref, self.tk = m_ref, acc_ref, tk

    def reset(self):
        self.m_ref[...] = jnp.full(self.m_ref.shape, NEG_INF, F32)
        self.acc_ref[...] = jnp.zeros(self.acc_ref.shape, F32)

    def _chunks(self):
        n_cols = self.m_ref.shape[1]
        return [slice(j, j + PIPE_CHUNK) for j in range(0, n_cols, PIPE_CHUNK)]

    def scores(self, slot, tile, bias=None):
        k0 = pl.multiple_of(tile * self.tk, self.tk)
        k_tile = self.k_ref[pl.ds(k0, self.tk), :]
        for cs in self._chunks():
            s = jnp.dot(k_tile, self.qm_ref[:, cs], preferred_element_type=F32)
            if bias is not None:
                b0 = cs.start % bias.shape[1]
                s = s + bias[:, b0:b0 + PIPE_CHUNK]
            self.s[slot][:, cs] = s
            self.c[slot][:, cs] = jnp.max(s, axis=0, keepdims=True)

    def softmax(self, slot, shift_const=None):
        for cs in self._chunks():
            m_prev = self.m_ref[:, cs]
            s_max = self.c[slot][:, cs]
            if shift_const is not None:
                m_new = jnp.maximum(m_prev, s_max + shift_const)
                shift = m_new - shift_const
            else:
                m_new = jnp.maximum(m_prev, s_max)
                shift = m_new
            self.p[slot][:, cs] = jnp.exp2(self.s[slot][:, cs] - shift).astype(BF16)
            self.a[slot][:, cs] = jnp.exp2(m_prev - m_new)
            self.m_ref[:, cs] = m_new

    def values(self, slot, tile):
        k0 = pl.multiple_of(tile * self.tk, self.tk)
        ones = jnp.ones((BF16_SUBLANES, self.tk), BF16)
        vt = jnp.concatenate([self.vt_ref[:, pl.ds(k0, self.tk)], ones], axis=0)
        for cs in self._chunks():
            self.acc_ref[:, cs] = (self.a[slot][:, cs] * self.acc_ref[:, cs]
                                   + jnp.dot(vt, self.p[slot][:, cs], preferred_element_type=F32))


def _pipe_scratch(tk, n_cols, dv):
    return [pltpu.VMEM((2 * HEAD_DIM, n_cols), BF16),
            pltpu.VMEM((tk, n_cols), F32), pltpu.VMEM((tk, n_cols), F32),
            pltpu.VMEM((1, n_cols), F32), pltpu.VMEM((1, n_cols), F32),
            pltpu.VMEM((tk, n_cols), BF16), pltpu.VMEM((tk, n_cols), BF16),
            pltpu.VMEM((1, n_cols), F32), pltpu.VMEM((1, n_cols), F32),
            pltpu.VMEM((1, n_cols), F32),
            pltpu.VMEM((dv + BF16_SUBLANES, n_cols), F32)]


def _attn_a_kernel(lam_ref, far_ref, qt_ref, k_ref, vt_ref, bias_ref, subln_ref, o_ref,
                   qm_ref, s0, s1, c0, c1, p0, p1, a0, a1, m_ref, acc_ref, *, seq, tile, out_scale):
    head = pl.program_id(0)
    n_tiles = seq // tile
    n_far = n_tiles - T5_WINDOW
    dv = 2 * HEAD_DIM
    lam = lam_ref[0]
    c_left = far_ref[head, 0]
    c_right = far_ref[head, 1]
    first_half = lax.broadcasted_iota(jnp.int32, (dv, tile), 0) < HEAD_DIM
    pipe = _Pipe(k_ref, vt_ref, qm_ref, s0, s1, c0, c1, p0, p1, a0, a1, m_ref, acc_ref, tile)

    def q_body(qi, carry):
        q0 = pl.multiple_of(qi * tile, tile)
        q_tile = qt_ref[:, pl.ds(q0, tile)]
        zeros = jnp.zeros_like(q_tile)
        qm_ref[...] = jnp.concatenate([jnp.where(first_half, q_tile, zeros),
                                       jnp.where(first_half, zeros, q_tile)], axis=1)
        pipe.reset()
        w0 = jnp.clip(qi - 1, 0, n_far)

        def far_tile(f):
            f = jnp.minimum(f, n_far - 1)
            return jnp.where(f < w0, f, f + T5_WINDOW)

        def window_scores(i):
            pipe.scores(i % 2, w0 + i, bias=bias_ref.at[0, w0 + i - qi + (T5_WINDOW - 1)])

        window_scores(0)
        for i in range(T5_WINDOW):
            if i > 0:
                pipe.values((i - 1) % 2, w0 + i - 1)
            pipe.softmax(i % 2)
            if i + 1 < T5_WINDOW:
                window_scores(i + 1)
            elif n_far > 0:
                pipe.scores((i + 1) % 2, far_tile(0))

        def far_step(f, c):
            prev = jnp.where(f == 0, w0 + T5_WINDOW - 1, far_tile(f - 1))
            shift = jnp.where(f < w0, c_left, c_right)
            nxt = far_tile(f + 1)
            for u in range(2):
                @pl.when((f & 1) == u)
                def _():
                    pipe.values(1 - u, prev)
                    pipe.softmax(u, shift_const=shift)
                    pipe.scores(1 - u, nxt)
            return c

        if n_far > 0:
            lax.fori_loop(0, n_far, far_step, 0)
            pipe.values(1, far_tile(n_far - 1))
        else:
            pipe.values(1, w0 + T5_WINDOW - 1)

        acc = acc_ref[...]
        acc = acc[:dv] * (1.0 / acc[dv:dv + 1])
        o = acc[:, :tile] - lam * acc[:, tile:]
        ms = jnp.mean(o * o, axis=0, keepdims=True)
        o = o * lax.rsqrt(ms + EPS) * subln_ref[...] * out_scale
        o_ref[pl.ds(q0, tile), :] = o.T.astype(BF16)
        return carry

    lax.fori_loop(0, n_tiles, q_body, 0)


def _attn_a(qt, k, vt, bias_tiles, far_const, lam, subln, batch, seq, out_scale):
    d = k.shape[1]
    hw = 2 * HEAD_DIM
    tile = bias_tiles.shape[-1]
    n_tiles = seq // tile
    assert n_tiles >= T5_WINDOW and n_tiles % 2 == 0
    smem = pl.BlockSpec(memory_space=pltpu.SMEM)
    return pl.pallas_call(
        functools.partial(_attn_a_kernel, seq=seq, tile=tile, out_scale=out_scale),
        grid=(d // hw, batch),
        in_specs=[smem, smem,
                  pl.BlockSpec((hw, seq), lambda h, b: (h, b)),
                  pl.BlockSpec((seq, hw), lambda h, b: (b, h)),
                  pl.BlockSpec((hw, seq), lambda h, b: (h, b)),
                  pl.BlockSpec((1, 2 * T5_WINDOW - 1, tile, tile), lambda h, b: (h, 0, 0, 0),
                               pipeline_mode=pl.Buffered(1)),
                  _const_spec((hw, 1))],
        out_specs=pl.BlockSpec((seq, hw), lambda h, b: (b, h)),
        out_shape=jax.ShapeDtypeStruct(k.shape, BF16),
        scratch_shapes=_pipe_scratch(tile, 2 * tile, hw),
        compiler_params=_params("parallel", "parallel"),
        name="attn_diff",
    )(lam, far_const, qt, k, vt, bias_tiles, subln)


def _t5_bias_vec(t5_table, span):
    nb = T5_BUCKETS // 2
    max_exact = nb // 2
    rel = jnp.arange(-(span - 1), span, dtype=jnp.int32)
    ret = jnp.where(rel > 0, nb, 0)
    n = jnp.abs(rel)
    n_f = jnp.maximum(n, 1).astype(F32)
    large = max_exact + (jnp.log(n_f / max_exact) / math.log(T5_MAX_DIST / max_exact)
                         * (nb - max_exact)).astype(jnp.int32)
    large = jnp.minimum(large, nb - 1)
    bucket = ret + jnp.where(n < max_exact, n, large)
    return t5_table.astype(F32)[bucket]


def _t5_tiles(t5_table, tile):
    assert tile > T5_MAX_DIST
    span = T5_WINDOW * tile
    vec_t = _t5_bias_vec(t5_table, span + 1).T * LOG2E
    period = 2 * tile
    y = np.arange(period)
    shift = np.where(y <= tile, -y, period - y)
    rel = np.stack([j * tile + shift for j in range(1 - T5_WINDOW, T5_WINDOW)])
    gen = vec_t[:, rel + span]
    flat = jnp.tile(gen, (1, 1, tile))[:, :, :tile * (period - 1)]
    tiles = flat.reshape(gen.shape[0], gen.shape[1], tile, period - 1)[:, :, :, :tile]
    far = jnp.stack([vec_t[:, 0], vec_t[:, -1]], axis=1)
    return tiles, far


def _attn_b_kernel(qt_ref, k_ref, vt_ref, o_ref, qm_ref, s0, s1, c0, c1, p0, p1, a0, a1, m_ref, acc_ref,
                   *, seq, tq, tk, group):
    upper = pl.program_id(0) % 2 == 1
    n_tiles = seq // tk
    pipe = _Pipe(k_ref, vt_ref, qm_ref, s0, s1, c0, c1, p0, p1, a0, a1, m_ref, acc_ref, tk)

    def q_body(qi, carry):
        q0 = pl.multiple_of(qi * tq, tq)
        q_tile = qt_ref[:, pl.ds(q0, tq)]
        q_cat = jnp.concatenate([q_tile[r * HEAD_DIM:(r + 1) * HEAD_DIM] for r in range(group)],
                                axis=1)
        zeros = jnp.zeros_like(q_cat)
        qm_ref[...] = jnp.concatenate([jnp.where(upper, zeros, q_cat),
                                       jnp.where(upper, q_cat, zeros)], axis=0)
        pipe.reset()
        pipe.scores(0, 0)
        pipe.softmax(0)
        pipe.scores(1, 1)
        pipe.values(0, 0)
        pipe.softmax(1)
        pipe.scores(0, min(2, n_tiles - 1))

        def step(t, c):
            nxt = jnp.minimum(t + 1, n_tiles - 1)
            for u in range(2):
                @pl.when((t & 1) == u)
                def _():
                    pipe.values(1 - u, t - 1)
                    pipe.softmax(u)
                    pipe.scores(1 - u, nxt)
            return c

        lax.fori_loop(2, n_tiles, step, 0)
        pipe.values(1, n_tiles - 1)
        acc = acc_ref[...]
        o = acc[:HEAD_DIM] * (1.0 / acc[HEAD_DIM:HEAD_DIM + 1])
        o_cat = jnp.concatenate([o[:, r * tq:(r + 1) * tq] for r in range(group)], axis=0)
        o_ref[pl.ds(q0, tq), :] = o_cat.T.astype(BF16)
        return carry

    lax.fori_loop(0, seq // tq, q_body, 0)


def _attn_b(qt, k, vt, batch, seq):
    dq = qt.shape[0]
    n_kv = vt.shape[0] // HEAD_DIM
    group = dq // HEAD_DIM // n_kv
    gw = group * HEAD_DIM
    tq, tk = ATTN_B_TQ, ATTN_B_TK
    assert seq % (2 * tk) == 0 and seq % tq == 0
    return pl.pallas_call(
        functools.partial(_attn_b_kernel, seq=seq, tq=tq, tk=tk, group=group),
        grid=(n_kv, batch),
        in_specs=[pl.BlockSpec((gw, seq), lambda g, b: (g, b)),
                  pl.BlockSpec((seq, 2 * HEAD_DIM), lambda g, b: (b, g // 2)),
                  pl.BlockSpec((HEAD_DIM, seq), lambda g, b: (g, b))],
        out_specs=pl.BlockSpec((seq, gw), lambda g, b: (b, g)),
        out_shape=jax.ShapeDtypeStruct((batch * seq, dq), BF16),
        scratch_shapes=_pipe_scratch(tk, group * tq, HEAD_DIM),
        compiler_params=_params("parallel", "parallel"),
        name="attn_gqa",
    )(qt, k, vt)


def _rope_tables(seq):
    n = HEAD_DIM // 4
    t = jnp.arange(seq, dtype=jnp.int32)
    freqs = ROPE_THETA ** (-jnp.arange(n, dtype=F32) / n)

    def half(pos):
        ang = pos.astype(F32)[None, :] * freqs[:, None]
        c, s = jnp.cos(ang), jnp.sin(ang)
        return jnp.concatenate([c, c], axis=0), jnp.concatenate([-s, s], axis=0)

    c_row, s_row = half(t // GRID_W)
    c_col, s_col = half(t % GRID_W)
    return jnp.concatenate([c_row, c_col], axis=0), jnp.concatenate([s_row, s_col], axis=0)


def _attn_c_kernel(qt_ref, k_ref, vt_ref, bm_ref, o_ref, *, rows):
    j = pl.program_id(2)
    n_keys = NA_K_ROWS * GRID_W
    w0 = jnp.clip(j * NA_Q_ROWS - NA_ROWS // 2, 0, rows - NA_K_ROWS)
    k0 = pl.multiple_of(w0 * GRID_W, NA_ROWS // 2 * GRID_W)
    k_win = k_ref[pl.ds(k0, n_keys), :]
    q_tile = qt_ref[...]
    zeros = jnp.zeros_like(q_tile)
    first_head = lax.broadcasted_iota(jnp.int32, q_tile.shape, 0) < HEAD_DIM
    ones = jnp.ones((BF16_SUBLANES, n_keys), BF16)
    outs = []
    for hh in range(2):
        q_pad = jnp.where(first_head, q_tile, zeros) if hh == 0 else jnp.where(first_head, zeros, q_tile)
        s = jnp.dot(k_win, q_pad, preferred_element_type=F32) + bm_ref[hh, 0]
        p = jnp.exp2(s - jnp.max(s, axis=0, keepdims=True)).astype(BF16)
        v_win = jnp.concatenate([vt_ref[hh * HEAD_DIM:(hh + 1) * HEAD_DIM, pl.ds(k0, n_keys)], ones],
                                axis=0)
        acc = jnp.dot(v_win, p, preferred_element_type=F32)
        outs.append(acc[:HEAD_DIM] * (1.0 / acc[HEAD_DIM:HEAD_DIM + 1]))
    o_ref[...] = jnp.concatenate(outs, axis=0).T.astype(BF16)


def _attn_c(qt, k, vt, biasmask, batch, seq):
    d = k.shape[1]
    hw = 2 * HEAD_DIM
    rows = seq // GRID_W
    assert rows >= NA_K_ROWS and rows % NA_Q_ROWS == 0
    n_blk = rows // NA_Q_ROWS
    n_q = NA_Q_ROWS * GRID_W
    n_keys = NA_K_ROWS * GRID_W

    def variant(j):
        return jnp.where(j == 0, 0, jnp.where(j == n_blk - 1, 2, 1))

    return pl.pallas_call(
        functools.partial(_attn_c_kernel, rows=rows),
        grid=(d // hw, batch, n_blk),
        in_specs=[pl.BlockSpec((hw, n_q), lambda h, b, j: (h, b * n_blk + j)),
                  pl.BlockSpec((seq, hw), lambda h, b, j: (b, h)),
                  pl.BlockSpec((hw, seq), lambda h, b, j: (h, b)),
                  pl.BlockSpec((2, 1, n_keys, n_q), lambda h, b, j: (h, variant(j), 0, 0))],
        out_specs=pl.BlockSpec((n_q, hw), lambda h, b, j: (b * n_blk + j, h)),
        out_shape=jax.ShapeDtypeStruct(k.shape, BF16),
        compiler_params=_params("parallel", "parallel", "arbitrary"),
        name="attn_nbr",
    )(qt, k, vt, biasmask)


def _na_biasmask(rpb):
    rows = 4 * NA_Q_ROWS
    n_heads = rpb.shape[0]
    col = np.arange(GRID_W)
    c_start = np.clip(col - NA_COLS // 2, 0, GRID_W - NA_COLS)
    col_ok = (col[:, None] >= c_start[None, :]) & (col[:, None] < c_start[None, :] + NA_COLS)
    c_idx = np.clip(col[:, None] - col[None, :] + NA_COLS - 1, 0, 2 * NA_COLS - 2)
    by_col = jnp.take(rpb.astype(F32) * LOG2E, c_idx.reshape(-1), axis=2)
    n_blk = rows // NA_Q_ROWS
    tiles = []
    for j in (0, 1, n_blk - 1):
        w0 = min(max(j * NA_Q_ROWS - NA_ROWS // 2, 0), rows - NA_K_ROWS)
        kr = w0 + np.arange(NA_K_ROWS)
        qr = j * NA_Q_ROWS + np.arange(NA_Q_ROWS)
        r_start = np.clip(qr - NA_ROWS // 2, 0, rows - NA_ROWS)
        row_ok = (kr[:, None] >= r_start[None, :]) & (kr[:, None] < r_start[None, :] + NA_ROWS)
        r_idx = np.clip(kr[:, None] - qr[None, :] + NA_ROWS - 1, 0, 2 * NA_ROWS - 2)
        bias = jnp.take(by_col, r_idx.reshape(-1), axis=1)
        bias = bias.reshape(n_heads, NA_K_ROWS, NA_Q_ROWS, GRID_W, GRID_W)
        ok = row_ok[:, :, None, None] & col_ok[None, None, :, :]
        bias = jnp.where(ok[None], bias, NEG_INF)
        tiles.append(jnp.transpose(bias, (0, 1, 3, 2, 4)).reshape(
            n_heads, NA_K_ROWS * GRID_W, NA_Q_ROWS * GRID_W))
    return jnp.stack(tiles, axis=1)


def _trunk(x3, w):
    batch, seq, d = x3.shape
    x = x3.reshape(batch * seq, d)
    depth = w["ffn_w_in"].shape[0]
    for i in range(depth):
        j, kind = divmod(i, N_MIXERS)
        x = _ffn(x, w["ffn_norm"][i, 0], w["ffn_w_in"][i, 0], w["ffn_w_out"][i, 0], w["final_norm"], False)
        if kind == 0:
            lambda_init = 0.8 - 0.6 * math.exp(-0.3 * i)
            qt, k, vt = _proj(x, w["mix_norm"][i], *w["a_qkv"][j])
            lamf = w["a_lambda"][j].astype(F32)
            lam = (jnp.exp(jnp.sum(lamf[0] * lamf[1])) - jnp.exp(jnp.sum(lamf[2] * lamf[3]))
                   + lambda_init).reshape(1)
            o = _attn_a(qt, k, vt, *w["t5"], lam, w["a_subln"][j], batch, seq, 1.0 - lambda_init)
            w_out = w["a_w_out"][j]
        elif kind == 1:
            cos_t, sin_t = _rope_tables(seq)
            qt, k, vt = _proj_b(x, w["mix_norm"][i], *w["b_qkv"][j], w["b_q_norm"][j], w["b_k_norm"][j],
                                cos_t, sin_t)
            o = _attn_b(qt, k, vt, batch, seq)
            w_out = w["b_w_out"][j]
        else:
            qt, k, vt = _proj(x, w["mix_norm"][i], *w["c_qkv"][j])
            o = _attn_c(qt, k, vt, w["c_biasmask"][j], batch, seq)
            w_out = w["c_w_out"][j]
        x = _out_proj(x, o, w_out)
        x = _ffn(x, w["ffn_norm"][i, 1], w["ffn_w_in"][i, 1], w["ffn_w_out"][i, 1], w["final_norm"],
                 i == depth - 1)
    return x.reshape(batch, seq, d)


def _split_qkv_t(w_qkv, n_q, n_k):
    wq = jnp.swapaxes(w_qkv[:, :, :n_q], 1, 2).astype(BF16)
    wk = w_qkv[:, :, n_q:n_q + n_k].astype(BF16)
    wv = jnp.swapaxes(w_qkv[:, :, n_q + n_k:], 1, 2).astype(BF16)
    return [(wq[j], wk[j], wv[j]) for j in range(w_qkv.shape[0])]


def kernel(x_prompt, x_sample, ffn_norm, ffn_w_in, ffn_w_out, mix_norm, a_w_qkv, a_lambda, a_subln, a_w_out,
           t5_table, b_w_qkv, b_q_norm, b_k_norm, b_w_out, c_w_qkv, c_rpb, c_w_out, final_norm):
    d = x_prompt.shape[-1]
    n_kv = (b_w_qkv.shape[-1] - d) // 2
    b_qkv = _split_qkv_t(b_w_qkv, d, n_kv)
    w = {
        "ffn_norm": ffn_norm.astype(F32)[:, :, None, :],
        "ffn_w_in": ffn_w_in.astype(BF16),
        "ffn_w_out": ffn_w_out.astype(BF16),
        "mix_norm": mix_norm.astype(F32)[:, None, :],
        "final_norm": final_norm.astype(F32)[None, :],
        "a_qkv": _split_qkv_t(a_w_qkv, d, d),
        "a_lambda": a_lambda,
        "a_subln": a_subln.astype(F32)[:, :, None],
        "a_w_out": a_w_out.astype(BF16),
        "t5": _t5_tiles(t5_table, ATTN_A_TILE),
        "b_qkv": [(wq, jnp.swapaxes(wk, 0, 1), wv) for wq, wk, wv in b_qkv],
        "b_q_norm": b_q_norm.astype(F32)[:, :, None],
        "b_k_norm": b_k_norm.astype(F32)[:, :, None],
        "b_w_out": b_w_out.astype(BF16),
        "c_qkv": _split_qkv_t(c_w_qkv, d, d),
        "c_biasmask": [_na_biasmask(c_rpb[j]) for j in range(c_rpb.shape[0])],
        "c_w_out": c_w_out.astype(BF16),
    }
    return (_trunk(x_prompt, w), _trunk(x_sample, w))
```

```python
import functools
import math

import numpy as np
import jax
import jax.numpy as jnp
from jax import lax
from jax.experimental import pallas as pl
from jax.experimental.pallas import tpu as pltpu

F32 = jnp.float32
BF16 = jnp.bfloat16

EPS = 1e-6
HEAD_DIM = 64
GRID_W = 64
NEG_INF = -1e30
N_MIXERS = 3
T5_BUCKETS = 32
T5_MAX_DIST = 128
ROPE_THETA = 10000.0
NA_ROWS = 8
NA_COLS = 16
LOG2E = math.log2(math.e)
Q_SCALE = HEAD_DIM ** -0.5 * LOG2E

V7X_VMEM_BYTES = 64 * 1024 * 1024
VMEM_LIMIT_BYTES = V7X_VMEM_BYTES - 8 * 1024 * 1024
MXU_DIM = 256
BF16_SUBLANES = 16
ROW_TILE = 512
FFN_CHUNK = MXU_DIM
ATTN_A_TILE = 512
PIPE_CHUNK = 256
GUARD_LOG2 = 64.0
T5_WINDOW = 4
ATTN_B_TQ = 256
ATTN_B_TK = 512
NA_Q_ROWS = 8
NA_K_ROWS = 16


def _params(*sem):
    return pltpu.CompilerParams(dimension_semantics=sem, vmem_limit_bytes=VMEM_LIMIT_BYTES)


def _const_spec(shape):
    nd = len(shape)
    return pl.BlockSpec(shape, lambda *_: (0,) * nd, pipeline_mode=pl.Buffered(1))


def _rms(x, gain):
    ms = jnp.mean(x * x, axis=-1, keepdims=True)
    return x * lax.rsqrt(ms + EPS) * gain


def _ffn_kernel(x_ref, g_ref, wi_ref, wo_ref, fg_ref, o_ref, act_ref, *, d_ff, final):
    x = x_ref[...]
    xn = _rms(x, g_ref[...]).astype(BF16)
    for c in range(d_ff // FFN_CHUNK):
        lo = c * FFN_CHUNK
        gate = jnp.dot(xn, wi_ref[:, lo:lo + FFN_CHUNK], preferred_element_type=F32)
        up = jnp.dot(xn, wi_ref[:, d_ff + lo:d_ff + lo + FFN_CHUNK], preferred_element_type=F32)
        act_ref[:, lo:lo + FFN_CHUNK] = (gate * jax.nn.sigmoid(gate) * up).astype(BF16)
    y = x + 0.5 * jnp.dot(act_ref[...], wo_ref[...], preferred_element_type=F32)
    if final:
        y = _rms(y, fg_ref[...])
    o_ref[...] = y


def _ffn(x, gain, w_in, w_out, final_gain, final):
    t, d = x.shape
    d_ff = w_out.shape[0]
    row = pl.BlockSpec((ROW_TILE, d), lambda i: (i, 0))
    return pl.pallas_call(
        functools.partial(_ffn_kernel, d_ff=d_ff, final=final),
        grid=(t // ROW_TILE,),
        in_specs=[row, _const_spec((1, d)), _const_spec(w_in.shape), _const_spec(w_out.shape),
                  _const_spec((1, d))],
        out_specs=row,
        out_shape=jax.ShapeDtypeStruct((t, d), F32),
        scratch_shapes=[pltpu.VMEM((ROW_TILE, d_ff), BF16)],
        compiler_params=_params("parallel"),
        name="ffn",
    )(x, gain, w_in, w_out, final_gain)


def _dot_nt(a, b):
    return lax.dot_general(a, b, (((1,), (1,)), ((), ())), preferred_element_type=F32)


def _proj_kernel(x_ref, g_ref, wqt_ref, wk_ref, wvt_ref, qt_ref, k_ref, vt_ref):
    hn = _rms(x_ref[...], g_ref[...]).astype(BF16)
    qt_ref[...] = (_dot_nt(wqt_ref[...], hn) * Q_SCALE).astype(BF16)
    k_ref[...] = jnp.dot(hn, wk_ref[...], preferred_element_type=F32).astype(BF16)
    vt_ref[...] = _dot_nt(wvt_ref[...], hn).astype(BF16)


def _proj(x, gain, wqt, wk, wvt):
    t, d = x.shape
    row = pl.BlockSpec((ROW_TILE, d), lambda i: (i, 0))
    col = pl.BlockSpec((d, ROW_TILE), lambda i: (0, i))
    return pl.pallas_call(
        _proj_kernel,
        grid=(t // ROW_TILE,),
        in_specs=[row, _const_spec((1, d)), _const_spec(wqt.shape), _const_spec(wk.shape),
                  _const_spec(wvt.shape)],
        out_specs=[col, row, col],
        out_shape=[jax.ShapeDtypeStruct((d, t), BF16), jax.ShapeDtypeStruct((t, d), BF16),
                   jax.ShapeDtypeStruct((d, t), BF16)],
        compiler_params=_params("parallel"),
        name="proj_qkv",
    )(x, gain, wqt, wk, wvt)


def _norm_rope_t(yt, gain, cos, sin, n_heads):
    q4 = HEAD_DIM // 4
    out = []
    for h in range(n_heads):
        y = yt[h * HEAD_DIM:(h + 1) * HEAD_DIM]
        ms = jnp.mean(y * y, axis=0, keepdims=True)
        yn = y * lax.rsqrt(ms + EPS) * gain
        swapped = jnp.concatenate([yn[q4:2 * q4], yn[:q4], yn[3 * q4:], yn[2 * q4:3 * q4]], axis=0)
        out.append(yn * cos + swapped * sin)
    return jnp.concatenate(out, axis=0)


def _proj_b_kernel(x_ref, g_ref, wqt_ref, wkt_ref, wvt_ref, qn_ref, kn_ref, cos_ref, sin_ref,
                   qt_ref, k_ref, vt_ref, *, n_q, n_kv):
    hn = _rms(x_ref[...], g_ref[...]).astype(BF16)
    cos = cos_ref[...]
    sin = sin_ref[...]
    q = _norm_rope_t(_dot_nt(wqt_ref[...], hn), qn_ref[...], cos, sin, n_q)
    qt_ref[...] = (q * Q_SCALE).astype(BF16)
    k = _norm_rope_t(_dot_nt(wkt_ref[...], hn), kn_ref[...], cos, sin, n_kv)
    k_ref[...] = k.T.astype(BF16)
    vt_ref[...] = _dot_nt(wvt_ref[...], hn).astype(BF16)


def _proj_b(x, gain, wqt, wkt, wvt, q_norm, k_norm, cos_t, sin_t):
    t, d = x.shape
    dq, dkv = wqt.shape[0], wkt.shape[0]
    s_tiles = cos_t.shape[1] // ROW_TILE
    row = pl.BlockSpec((ROW_TILE, d), lambda i: (i, 0))
    tab = pl.BlockSpec((HEAD_DIM, ROW_TILE), lambda i: (0, i % s_tiles))
    return pl.pallas_call(
        functools.partial(_proj_b_kernel, n_q=dq // HEAD_DIM, n_kv=dkv // HEAD_DIM),
        grid=(t // ROW_TILE,),
        in_specs=[row, _const_spec((1, d)), _const_spec(wqt.shape), _const_spec(wkt.shape),
                  _const_spec(wvt.shape), _const_spec((HEAD_DIM, 1)), _const_spec((HEAD_DIM, 1)),
                  tab, tab],
        out_specs=[pl.BlockSpec((dq, ROW_TILE), lambda i: (0, i)),
                   pl.BlockSpec((ROW_TILE, dkv), lambda i: (i, 0)),
                   pl.BlockSpec((dkv, ROW_TILE), lambda i: (0, i))],
        out_shape=[jax.ShapeDtypeStruct((dq, t), BF16), jax.ShapeDtypeStruct((t, dkv), BF16),
                   jax.ShapeDtypeStruct((dkv, t), BF16)],
        compiler_params=_params("parallel"),
        name="proj_qkv_rope",
    )(x, gain, wqt, wkt, wvt, q_norm, k_norm, cos_t, sin_t)


def _out_proj_kernel(x_ref, o_ref, w_ref, y_ref):
    y_ref[...] = x_ref[...] + jnp.dot(o_ref[...], w_ref[...], preferred_element_type=F32)


def _out_proj(x, o, w):
    t, d = x.shape
    row = pl.BlockSpec((ROW_TILE, d), lambda i: (i, 0))
    return pl.pallas_call(
        _out_proj_kernel,
        grid=(t // ROW_TILE,),
        in_specs=[row, row, _const_spec(w.shape)],
        out_specs=row,
        out_shape=jax.ShapeDtypeStruct((t, d), F32),
        compiler_params=_params("parallel"),
        name="out_proj",
    )(x, o, w)


class _Pipe:
    def __init__(self, k_ref, vt_ref, qm_ref, p0, p1, a0, a1, b0, b1, m_ref, acc_ref, g_ref, tk):
        self.k_ref, self.vt_ref, self.qm_ref = k_ref, vt_ref, qm_ref
        self.p, self.a, self.b = (p0, p1), (a0, a1), (b0, b1)
        self.m_ref, self.acc_ref, self.g_ref, self.tk = m_ref, acc_ref, g_ref, tk

    def reset(self):
        self.m_ref[...] = jnp.full(self.m_ref.shape, NEG_INF, F32)
        self.acc_ref[...] = jnp.zeros(self.acc_ref.shape, F32)

    def _chunks(self):
        n_cols = self.m_ref.shape[1]
        return [slice(j, j + PIPE_CHUNK) for j in range(0, n_cols, PIPE_CHUNK)]

    def _scores(self, k_tile, cs, bias):
        s = jnp.dot(k_tile, self.qm_ref[:, cs], preferred_element_type=F32)
        if bias is not None:
            b0 = cs.start % bias.shape[1]
            s = s + bias[:, b0:b0 + PIPE_CHUNK]
        return s

    def expo(self, slot, tile, bias=None, shift_const=None):
        k0 = pl.multiple_of(tile * self.tk, self.tk)
        k_tile = self.k_ref[pl.ds(k0, self.tk), :]
        jump = None
        for cs in self._chunks():
            s = self._scores(k_tile, cs, bias)
            m_prev = self.m_ref[:, cs]
            m_tile = jnp.max(s, axis=0, keepdims=True)
            if shift_const is not None:
                m_tile = m_tile + shift_const
                self.p[slot][:, cs] = jnp.exp2(s - (m_prev - shift_const)).astype(BF16)
            else:
                self.p[slot][:, cs] = jnp.exp2(s - m_prev).astype(BF16)
            m_new = jnp.maximum(m_prev, m_tile)
            alpha = jnp.exp2(m_prev - m_new)
            self.a[slot][:, cs] = alpha
            self.b[slot][:, cs] = alpha
            self.m_ref[:, cs] = m_new
            gap = jnp.max(m_tile - m_prev)
            jump = gap if jump is None else jnp.maximum(jump, gap)
        self.g_ref[slot] = (jump > GUARD_LOG2).astype(jnp.int32)

    def recompute(self, slot, tile, bias=None, shift_const=None):
        @pl.when(self.g_ref[slot] != 0)
        def _():
            k0 = pl.multiple_of(tile * self.tk, self.tk)
            k_tile = self.k_ref[pl.ds(k0, self.tk), :]
            for cs in self._chunks():
                s = self._scores(k_tile, cs, bias)
                m_new = self.m_ref[:, cs]
                if shift_const is not None:
                    m_new = m_new - shift_const
                self.p[slot][:, cs] = jnp.exp2(s - m_new).astype(BF16)
                self.b[slot][:, cs] = jnp.ones((1, PIPE_CHUNK), F32)
            self.g_ref[slot] = jnp.int32(0)

    def values(self, slot, tile):
        k0 = pl.multiple_of(tile * self.tk, self.tk)
        ones = jnp.ones((BF16_SUBLANES, self.tk), BF16)
        vt = jnp.concatenate([self.vt_ref[:, pl.ds(k0, self.tk)], ones], axis=0)
        for cs in self._chunks():
            pv = jnp.dot(vt, self.p[slot][:, cs], preferred_element_type=F32)
            self.acc_ref[:, cs] = self.a[slot][:, cs] * self.acc_ref[:, cs] + self.b[slot][:, cs] * pv


def _pipe_scratch(tk, n_cols, dv):
    return [pltpu.VMEM((2 * HEAD_DIM, n_cols), BF16),
            pltpu.VMEM((tk, n_cols), BF16), pltpu.VMEM((tk, n_cols), BF16),
            pltpu.VMEM((1, n_cols), F32), pltpu.VMEM((1, n_cols), F32),
            pltpu.VMEM((1, n_cols), F32), pltpu.VMEM((1, n_cols), F32),
            pltpu.VMEM((1, n_cols), F32),
            pltpu.VMEM((dv + BF16_SUBLANES, n_cols), F32),
            pltpu.SMEM((2,), jnp.int32)]


def _attn_a_kernel(lam_ref, far_ref, qt_ref, k_ref, vt_ref, bias_ref, subln_ref, o_ref,
                   qm_ref, p0, p1, a0, a1, b0, b1, m_ref, acc_ref, g_ref, *, seq, tile, out_scale):
    head = pl.program_id(0)
    n_tiles = seq // tile
    n_far = n_tiles - T5_WINDOW
    dv = 2 * HEAD_DIM
    lam = lam_ref[0]
    c_left = far_ref[head, 0]
    c_right = far_ref[head, 1]
    first_half = lax.broadcasted_iota(jnp.int32, (dv, tile), 0) < HEAD_DIM
    pipe = _Pipe(k_ref, vt_ref, qm_ref, p0, p1, a0, a1, b0, b1, m_ref, acc_ref, g_ref, tile)

    def q_body(qi, carry):
        q0 = pl.multiple_of(qi * tile, tile)
        q_tile = qt_ref[:, pl.ds(q0, tile)]
        zeros = jnp.zeros_like(q_tile)
        qm_ref[...] = jnp.concatenate([jnp.where(first_half, q_tile, zeros),
                                       jnp.where(first_half, zeros, q_tile)], axis=1)
        pipe.reset()
        w0 = jnp.clip(qi - 1, 0, n_far)

        def far_tile(f):
            f = jnp.minimum(f, n_far - 1)
            return jnp.where(f < w0, f, f + T5_WINDOW)

        def window_bias(i):
            return bias_ref.at[0, w0 + i - qi + (T5_WINDOW - 1)]

        for i in range(T5_WINDOW):
            if i > 0:
                pipe.recompute((i - 1) % 2, w0 + i - 1, bias=window_bias(i - 1))
                pipe.values((i - 1) % 2, w0 + i - 1)
            pipe.expo(i % 2, w0 + i, bias=window_bias(i))
        last_w = T5_WINDOW - 1
        pipe.recompute(last_w % 2, w0 + last_w, bias=window_bias(last_w))

        def far_step(f, c):
            prev = jnp.where(f == 0, w0 + last_w, far_tile(f - 1))
            prev_shift = jnp.where(f - 1 < w0, c_left, c_right)
            shift = jnp.where(f < w0, c_left, c_right)
            cur = far_tile(f)
            for u in range(2):
                @pl.when((f & 1) == u)
                def _():
                    pipe.recompute(1 - u, prev, shift_const=prev_shift)
                    pipe.expo(u, cur, shift_const=shift)
                    pipe.values(1 - u, prev)
            return c

        if n_far > 0:
            lax.fori_loop(0, n_far, far_step, 0)
            last = far_tile(n_far - 1)
            pipe.recompute(1, last, shift_const=jnp.where(n_far - 1 < w0, c_left, c_right))
            pipe.values(1, last)
        else:
            pipe.values(1, w0 + last_w)

        acc = acc_ref[...]
        acc = acc[:dv] * (1.0 / acc[dv:dv + 1])
        o = acc[:, :tile] - lam * acc[:, tile:]
        ms = jnp.mean(o * o, axis=0, keepdims=True)
        o = o * lax.rsqrt(ms + EPS) * subln_ref[...] * out_scale
        o_ref[pl.ds(q0, tile), :] = o.T.astype(BF16)
        return carry

    lax.fori_loop(0, n_tiles, q_body, 0)


def _attn_a(qt, k, vt, bias_tiles, far_const, lam, subln, batch, seq, out_scale):
    d = k.shape[1]
    hw = 2 * HEAD_DIM
    tile = bias_tiles.shape[-1]
    n_tiles = seq // tile
    assert n_tiles >= T5_WINDOW and n_tiles % 2 == 0
    smem = pl.BlockSpec(memory_space=pltpu.SMEM)
    return pl.pallas_call(
        functools.partial(_attn_a_kernel, seq=seq, tile=tile, out_scale=out_scale),
        grid=(d // hw, batch),
        in_specs=[smem, smem,
                  pl.BlockSpec((hw, seq), lambda h, b: (h, b)),
                  pl.BlockSpec((seq, hw), lambda h, b: (b, h)),
                  pl.BlockSpec((hw, seq), lambda h, b: (h, b)),
                  pl.BlockSpec((1, 2 * T5_WINDOW - 1, tile, tile), lambda h, b: (h, 0, 0, 0),
                               pipeline_mode=pl.Buffered(1)),
                  _const_spec((hw, 1))],
        out_specs=pl.BlockSpec((seq, hw), lambda h, b: (b, h)),
        out_shape=jax.ShapeDtypeStruct(k.shape, BF16),
        scratch_shapes=_pipe_scratch(tile, 2 * tile, hw),
        compiler_params=_params("parallel", "parallel"),
        name="attn_diff",
    )(lam, far_const, qt, k, vt, bias_tiles, subln)


def _t5_bias_vec(t5_table, span):
    nb = T5_BUCKETS // 2
    max_exact = nb // 2
    rel = jnp.arange(-(span - 1), span, dtype=jnp.int32)
    ret = jnp.where(rel > 0, nb, 0)
    n = jnp.abs(rel)
    n_f = jnp.maximum(n, 1).astype(F32)
    large = max_exact + (jnp.log(n_f / max_exact) / math.log(T5_MAX_DIST / max_exact)
                         * (nb - max_exact)).astype(jnp.int32)
    large = jnp.minimum(large, nb - 1)
    bucket = ret + jnp.where(n < max_exact, n, large)
    return t5_table.astype(F32)[bucket]


def _t5_tiles(t5_table, tile):
    assert tile > T5_MAX_DIST
    span = T5_WINDOW * tile
    vec_t = _t5_bias_vec(t5_table, span + 1).T * LOG2E
    period = 2 * tile
    y = np.arange(period)
    shift = np.where(y <= tile, -y, period - y)
    rel = np.stack([j * tile + shift for j in range(1 - T5_WINDOW, T5_WINDOW)])
    gen = vec_t[:, rel + span]
    flat = jnp.tile(gen, (1, 1, tile))[:, :, :tile * (period - 1)]
    tiles = flat.reshape(gen.shape[0], gen.shape[1], tile, period - 1)[:, :, :, :tile]
    far = jnp.stack([vec_t[:, 0], vec_t[:, -1]], axis=1)
    return tiles, far


def _attn_b_kernel(qt_ref, k_ref, vt_ref, o_ref, qm_ref, p0, p1, a0, a1, b0, b1, m_ref, acc_ref, g_ref,
                   *, seq, tq, tk, group):
    upper = pl.program_id(0) % 2 == 1
    n_tiles = seq // tk
    pipe = _Pipe(k_ref, vt_ref, qm_ref, p0, p1, a0, a1, b0, b1, m_ref, acc_ref, g_ref, tk)

    def q_body(qi, carry):
        q0 = pl.multiple_of(qi * tq, tq)
        q_tile = qt_ref[:, pl.ds(q0, tq)]
        q_cat = jnp.concatenate([q_tile[r * HEAD_DIM:(r + 1) * HEAD_DIM] for r in range(group)],
                                axis=1)
        zeros = jnp.zeros_like(q_cat)
        qm_ref[...] = jnp.concatenate([jnp.where(upper, zeros, q_cat),
                                       jnp.where(upper, q_cat, zeros)], axis=0)
        pipe.reset()
        pipe.expo(0, 0)

        def step(t, c):
            for u in range(2):
                @pl.when((t & 1) == u)
                def _():
                    pipe.recompute(1 - u, t - 1)
                    pipe.expo(u, t)
                    pipe.values(1 - u, t - 1)
            return c

        lax.fori_loop(1, n_tiles, step, 0)
        pipe.recompute(1, n_tiles - 1)
        pipe.values(1, n_tiles - 1)
        acc = acc_ref[...]
        o = acc[:HEAD_DIM] * (1.0 / acc[HEAD_DIM:HEAD_DIM + 1])
        o_cat = jnp.concatenate([o[:, r * tq:(r + 1) * tq] for r in range(group)], axis=0)
        o_ref[pl.ds(q0, tq), :] = o_cat.T.astype(BF16)
        return carry

    lax.fori_loop(0, seq // tq, q_body, 0)


def _attn_b(qt, k, vt, batch, seq):
    dq = qt.shape[0]
    n_kv = vt.shape[0] // HEAD_DIM
    group = dq // HEAD_DIM // n_kv
    gw = group * HEAD_DIM
    tq, tk = ATTN_B_TQ, ATTN_B_TK
    assert seq % (2 * tk) == 0 and seq % tq == 0
    return pl.pallas_call(
        functools.partial(_attn_b_kernel, seq=seq, tq=tq, tk=tk, group=group),
        grid=(n_kv, batch),
        in_specs=[pl.BlockSpec((gw, seq), lambda g, b: (g, b)),
                  pl.BlockSpec((seq, 2 * HEAD_DIM), lambda g, b: (b, g // 2)),
                  pl.BlockSpec((HEAD_DIM, seq), lambda g, b: (g, b))],
        out_specs=pl.BlockSpec((seq, gw), lambda g, b: (b, g)),
        out_shape=jax.ShapeDtypeStruct((batch * seq, dq), BF16),
        scratch_shapes=_pipe_scratch(tk, group * tq, HEAD_DIM),
        compiler_params=_params("parallel", "parallel"),
        name="attn_gqa",
    )(qt, k, vt)


def _rope_tables(seq):
    n = HEAD_DIM // 4
    t = jnp.arange(seq, dtype=jnp.int32)
    freqs = ROPE_THETA ** (-jnp.arange(n, dtype=F32) / n)

    def half(pos):
        ang = pos.astype(F32)[None, :] * freqs[:, None]
        c, s = jnp.cos(ang), jnp.sin(ang)
        return jnp.concatenate([c, c], axis=0), jnp.concatenate([-s, s], axis=0)

    c_row, s_row = half(t // GRID_W)
    c_col, s_col = half(t % GRID_W)
    return jnp.concatenate([c_row, c_col], axis=0), jnp.concatenate([s_row, s_col], axis=0)


def _attn_c_kernel(qt_ref, k_ref, vt_ref, bm_ref, o_ref, *, rows):
    j = pl.program_id(2)
    n_keys = NA_K_ROWS * GRID_W
    w0 = jnp.clip(j * NA_Q_ROWS - NA_ROWS // 2, 0, rows - NA_K_ROWS)
    k0 = pl.multiple_of(w0 * GRID_W, NA_ROWS // 2 * GRID_W)
    k_win = k_ref[pl.ds(k0, n_keys), :]
    q_tile = qt_ref[...]
    zeros = jnp.zeros_like(q_tile)
    first_head = lax.broadcasted_iota(jnp.int32, q_tile.shape, 0) < HEAD_DIM
    ones = jnp.ones((BF16_SUBLANES, n_keys), BF16)
    outs = []
    for hh in range(2):
        q_pad = jnp.where(first_head, q_tile, zeros) if hh == 0 else jnp.where(first_head, zeros, q_tile)
        s = jnp.dot(k_win, q_pad, preferred_element_type=F32) + bm_ref[hh, 0]
        p = jnp.exp2(s - jnp.max(s, axis=0, keepdims=True)).astype(BF16)
        v_win = jnp.concatenate([vt_ref[hh * HEAD_DIM:(hh + 1) * HEAD_DIM, pl.ds(k0, n_keys)], ones],
                                axis=0)
        acc = jnp.dot(v_win, p, preferred_element_type=F32)
        outs.append(acc[:HEAD_DIM] * (1.0 / acc[HEAD_DIM:HEAD_DIM + 1]))
    o_ref[...] = jnp.concatenate(outs, axis=0).T.astype(BF16)


def _attn_c(qt, k, vt, biasmask, batch, seq):
    d = k.shape[1]
    hw = 2 * HEAD_DIM
    rows = seq // GRID_W
    assert rows >= NA_K_ROWS and rows % NA_Q_ROWS == 0
    n_blk = rows // NA_Q_ROWS
    n_q = NA_Q_ROWS * GRID_W
    n_keys = NA_K_ROWS * GRID_W

    def variant(j):
        return jnp.where(j == 0, 0, jnp.where(j == n_blk - 1, 2, 1))

    return pl.pallas_call(
        functools.partial(_attn_c_kernel, rows=rows),
        grid=(d // hw, batch, n_blk),
        in_specs=[pl.BlockSpec((hw, n_q), lambda h, b, j: (h, b * n_blk + j)),
                  pl.BlockSpec((seq, hw), lambda h, b, j: (b, h)),
                  pl.BlockSpec((hw, seq), lambda h, b, j: (h, b)),
                  pl.BlockSpec((2, 1, n_keys, n_q), lambda h, b, j: (h, variant(j), 0, 0))],
        out_specs=pl.BlockSpec((n_q, hw), lambda h, b, j: (b * n_blk + j, h)),
        out_shape=jax.ShapeDtypeStruct(k.shape, BF16),
        compiler_params=_params("parallel", "parallel", "arbitrary"),
        name="attn_nbr",
    )(qt, k, vt, biasmask)


def _na_biasmask(rpb):
    rows = 4 * NA_Q_ROWS
    n_heads = rpb.shape[0]
    col = np.arange(GRID_W)
    c_start = np.clip(col - NA_COLS // 2, 0, GRID_W - NA_COLS)
    col_ok = (col[:, None] >= c_start[None, :]) & (col[:, None] < c_start[None, :] + NA_COLS)
    c_idx = np.clip(col[:, None] - col[None, :] + NA_COLS - 1, 0, 2 * NA_COLS - 2)
    by_col = jnp.take(rpb.astype(F32) * LOG2E, c_idx.reshape(-1), axis=2)
    n_blk = rows // NA_Q_ROWS
    tiles = []
    for j in (0, 1, n_blk - 1):
        w0 = min(max(j * NA_Q_ROWS - NA_ROWS // 2, 0), rows - NA_K_ROWS)
        kr = w0 + np.arange(NA_K_ROWS)
        qr = j * NA_Q_ROWS + np.arange(NA_Q_ROWS)
        r_start = np.clip(qr - NA_ROWS // 2, 0, rows - NA_ROWS)
        row_ok = (kr[:, None] >= r_start[None, :]) & (kr[:, None] < r_start[None, :] + NA_ROWS)
        r_idx = np.clip(kr[:, None] - qr[None, :] + NA_ROWS - 1, 0, 2 * NA_ROWS - 2)
        bias = jnp.take(by_col, r_idx.reshape(-1), axis=1)
        bias = bias.reshape(n_heads, NA_K_ROWS, NA_Q_ROWS, GRID_W, GRID_W)
        ok = row_ok[:, :, None, None] & col_ok[None, None, :, :]
        bias = jnp.where(ok[None], bias, NEG_INF)
        tiles.append(jnp.transpose(bias, (0, 1, 3, 2, 4)).reshape(
            n_heads, NA_K_ROWS * GRID_W, NA_Q_ROWS * GRID_W))
    return jnp.stack(tiles, axis=1)


def _trunk(x3, w):
    batch, seq, d = x3.shape
    x = x3.reshape(batch * seq, d)
    depth = w["ffn_w_in"].shape[0]
    for i in range(depth):
        j, kind = divmod(i, N_MIXERS)
        x = _ffn(x, w["ffn_norm"][i, 0], w["ffn_w_in"][i, 0], w["ffn_w_out"][i, 0], w["final_norm"], False)
        if kind == 0:
            lambda_init = 0.8 - 0.6 * math.exp(-0.3 * i)
            qt, k, vt = _proj(x, w["mix_norm"][i], *w["a_qkv"][j])
            lamf = w["a_lambda"][j].astype(F32)
            lam = (jnp.exp(jnp.sum(lamf[0] * lamf[1])) - jnp.exp(jnp.sum(lamf[2] * lamf[3]))
                   + lambda_init).reshape(1)
            o = _attn_a(qt, k, vt, *w["t5"], lam, w["a_subln"][j], batch, seq, 1.0 - lambda_init)
            w_out = w["a_w_out"][j]
        elif kind == 1:
            cos_t, sin_t = _rope_tables(seq)
            qt, k, vt = _proj_b(x, w["mix_norm"][i], *w["b_qkv"][j], w["b_q_norm"][j], w["b_k_norm"][j],
                                cos_t, sin_t)
            o = _attn_b(qt, k, vt, batch, seq)
            w_out = w["b_w_out"][j]
        else:
            qt, k, vt = _proj(x, w["mix_norm"][i], *w["c_qkv"][j])
            o = _attn_c(qt, k, vt, w["c_biasmask"][j], batch, seq)
            w_out = w["c_w_out"][j]
        x = _out_proj(x, o, w_out)
        x = _ffn(x, w["ffn_norm"][i, 1], w["ffn_w_in"][i, 1], w["ffn_w_out"][i, 1], w["final_norm"],
                 i == depth - 1)
    return x.reshape(batch, seq, d)


def _split_qkv_t(w_qkv, n_q, n_k):
    wq = jnp.swapaxes(w_qkv[:, :, :n_q], 1, 2).astype(BF16)
    wk = w_qkv[:, :, n_q:n_q + n_k].astype(BF16)
    wv = jnp.swapaxes(w_qkv[:, :, n_q + n_k:], 1, 2).astype(BF16)
    return [(wq[j], wk[j], wv[j]) for j in range(w_qkv.shape[0])]


def kernel(x_prompt, x_sample, ffn_norm, ffn_w_in, ffn_w_out, mix_norm, a_w_qkv, a_lambda, a_subln, a_w_out,
           t5_table, b_w_qkv, b_q_norm, b_k_norm, b_w_out, c_w_qkv, c_rpb, c_w_out, final_norm):
    d = x_prompt.shape[-1]
    n_kv = (b_w_qkv.shape[-1] - d) // 2
    b_qkv = _split_qkv_t(b_w_qkv, d, n_kv)
    w = {
        "ffn_norm": ffn_norm.astype(F32)[:, :, None, :],
        "ffn_w_in": ffn_w_in.astype(BF16),
        "ffn_w_out": ffn_w_out.astype(BF16),
        "mix_norm": mix_norm.astype(F32)[:, None, :],
        "final_norm": final_norm.astype(F32)[None, :],
        "a_qkv": _split_qkv_t(a_w_qkv, d, d),
        "a_lambda": a_lambda,
        "a_subln": a_subln.astype(F32)[:, :, None],
        "a_w_out": a_w_out.astype(BF16),
        "t5": _t5_tiles(t5_table, ATTN_A_TILE),
        "b_qkv": [(wq, jnp.swapaxes(wk, 0, 1), wv) for wq, wk, wv in b_qkv],
        "b_q_norm": b_q_norm.astype(F32)[:, :, None],
        "b_k_norm": b_k_norm.astype(F32)[:, :, None],
        "b_w_out": b_w_out.astype(BF16),
        "c_qkv": _split_qkv_t(c_w_qkv, d, d),
        "c_biasmask": [_na_biasmask(c_rpb[j]) for j in range(c_rpb.shape[0])],
        "c_w_out": c_w_out.astype(BF16),
    }
    return (_trunk(x_prompt, w), _trunk(x_sample, w))
```

```python
import functools
import math

import numpy as np
import jax
import jax.numpy as jnp
from jax import lax
from jax.experimental import pallas as pl
from jax.experimental.pallas import tpu as pltpu

F32 = jnp.float32
BF16 = jnp.bfloat16

EPS = 1e-6
HEAD_DIM = 64
GRID_W = 64
NEG_INF = -1e30
N_MIXERS = 3
T5_BUCKETS = 32
T5_MAX_DIST = 128
ROPE_THETA = 10000.0
NA_ROWS = 8
NA_COLS = 16
LOG2E = math.log2(math.e)
Q_SCALE = HEAD_DIM ** -0.5 * LOG2E

V7X_VMEM_BYTES = 64 * 1024 * 1024
VMEM_LIMIT_BYTES = V7X_VMEM_BYTES - 8 * 1024 * 1024
MXU_DIM = 256
BF16_SUBLANES = 16
ROW_TILE = 512
FFN_CHUNK = MXU_DIM
ATTN_A_TILE = 512
PIPE_CHUNK = 256
GUARD_LOG2 = 64.0
T5_WINDOW = 4
ATTN_B_TQ = 256
ATTN_B_TK = 512
NA_Q_ROWS = 8
NA_K_ROWS = 16


def _params(*sem):
    return pltpu.CompilerParams(dimension_semantics=sem, vmem_limit_bytes=VMEM_LIMIT_BYTES)


def _const_spec(shape):
    nd = len(shape)
    return pl.BlockSpec(shape, lambda *_: (0,) * nd, pipeline_mode=pl.Buffered(1))


def _rms(x, gain):
    ms = jnp.mean(x * x, axis=-1, keepdims=True)
    return x * lax.rsqrt(ms + EPS) * gain


def _ffn_kernel(x_ref, g_ref, wi_ref, wo_ref, fg_ref, o_ref, act_ref, *, d_ff, final):
    x = x_ref[...]
    xn = _rms(x, g_ref[...]).astype(BF16)
    for c in range(d_ff // FFN_CHUNK):
        lo = c * FFN_CHUNK
        gate = jnp.dot(xn, wi_ref[:, lo:lo + FFN_CHUNK], preferred_element_type=F32)
        up = jnp.dot(xn, wi_ref[:, d_ff + lo:d_ff + lo + FFN_CHUNK], preferred_element_type=F32)
        act_ref[:, lo:lo + FFN_CHUNK] = (gate * jax.nn.sigmoid(gate) * up).astype(BF16)
    y = x + 0.5 * jnp.dot(act_ref[...], wo_ref[...], preferred_element_type=F32)
    if final:
        y = _rms(y, fg_ref[...])
    o_ref[...] = y


def _ffn(x, gain, w_in, w_out, final_gain, final):
    t, d = x.shape
    d_ff = w_out.shape[0]
    row = pl.BlockSpec((ROW_TILE, d), lambda i: (i, 0))
    return pl.pallas_call(
        functools.partial(_ffn_kernel, d_ff=d_ff, final=final),
        grid=(t // ROW_TILE,),
        in_specs=[row, _const_spec((1, d)), _const_spec(w_in.shape), _const_spec(w_out.shape),
                  _const_spec((1, d))],
        out_specs=row,
        out_shape=jax.ShapeDtypeStruct((t, d), F32),
        scratch_shapes=[pltpu.VMEM((ROW_TILE, d_ff), BF16)],
        compiler_params=_params("parallel"),
        name="ffn",
    )(x, gain, w_in, w_out, final_gain)


def _dot_nt(a, b):
    return lax.dot_general(a, b, (((1,), (1,)), ((), ())), preferred_element_type=F32)


def _proj_kernel(x_ref, g_ref, wqt_ref, wk_ref, wvt_ref, qt_ref, k_ref, vt_ref):
    hn = _rms(x_ref[...], g_ref[...]).astype(BF16)
    qt_ref[...] = (_dot_nt(wqt_ref[...], hn) * Q_SCALE).astype(BF16)
    k_ref[...] = jnp.dot(hn, wk_ref[...], preferred_element_type=F32).astype(BF16)
    vt_ref[...] = _dot_nt(wvt_ref[...], hn).astype(BF16)


def _proj(x, gain, wqt, wk, wvt):
    t, d = x.shape
    row = pl.BlockSpec((ROW_TILE, d), lambda i: (i, 0))
    col = pl.BlockSpec((d, ROW_TILE), lambda i: (0, i))
    return pl.pallas_call(
        _proj_kernel,
        grid=(t // ROW_TILE,),
        in_specs=[row, _const_spec((1, d)), _const_spec(wqt.shape), _const_spec(wk.shape),
                  _const_spec(wvt.shape)],
        out_specs=[col, row, col],
        out_shape=[jax.ShapeDtypeStruct((d, t), BF16), jax.ShapeDtypeStruct((t, d), BF16),
                   jax.ShapeDtypeStruct((d, t), BF16)],
        compiler_params=_params("parallel"),
        name="proj_qkv",
    )(x, gain, wqt, wk, wvt)


def _norm_rope_t(yt, gain, cos, sin, n_heads):
    q4 = HEAD_DIM // 4
    out = []
    for h in range(n_heads):
        y = yt[h * HEAD_DIM:(h + 1) * HEAD_DIM]
        ms = jnp.mean(y * y, axis=0, keepdims=True)
        yn = y * lax.rsqrt(ms + EPS) * gain
        swapped = jnp.concatenate([yn[q4:2 * q4], yn[:q4], yn[3 * q4:], yn[2 * q4:3 * q4]], axis=0)
        out.append(yn * cos + swapped * sin)
    return jnp.concatenate(out, axis=0)


def _proj_b_kernel(x_ref, g_ref, wqt_ref, wkt_ref, wvt_ref, qn_ref, kn_ref, cos_ref, sin_ref,
                   qt_ref, k_ref, vt_ref, *, n_q, n_kv):
    hn = _rms(x_ref[...], g_ref[...]).astype(BF16)
    cos = cos_ref[...]
    sin = sin_ref[...]
    q = _norm_rope_t(_dot_nt(wqt_ref[...], hn), qn_ref[...], cos, sin, n_q)
    qt_ref[...] = (q * Q_SCALE).astype(BF16)
    k = _norm_rope_t(_dot_nt(wkt_ref[...], hn), kn_ref[...], cos, sin, n_kv)
    k_ref[...] = k.T.astype(BF16)
    vt_ref[...] = _dot_nt(wvt_ref[...], hn).astype(BF16)


def _proj_b(x, gain, wqt, wkt, wvt, q_norm, k_norm, cos_t, sin_t):
    t, d = x.shape
    dq, dkv = wqt.shape[0], wkt.shape[0]
    s_tiles = cos_t.shape[1] // ROW_TILE
    row = pl.BlockSpec((ROW_TILE, d), lambda i: (i, 0))
    tab = pl.BlockSpec((HEAD_DIM, ROW_TILE), lambda i: (0, i % s_tiles))
    return pl.pallas_call(
        functools.partial(_proj_b_kernel, n_q=dq // HEAD_DIM, n_kv=dkv // HEAD_DIM),
        grid=(t // ROW_TILE,),
        in_specs=[row, _const_spec((1, d)), _const_spec(wqt.shape), _const_spec(wkt.shape),
                  _const_spec(wvt.shape), _const_spec((HEAD_DIM, 1)), _const_spec((HEAD_DIM, 1)),
                  tab, tab],
        out_specs=[pl.BlockSpec((dq, ROW_TILE), lambda i: (0, i)),
                   pl.BlockSpec((ROW_TILE, dkv), lambda i: (i, 0)),
                   pl.BlockSpec((dkv, ROW_TILE), lambda i: (0, i))],
        out_shape=[jax.ShapeDtypeStruct((dq, t), BF16), jax.ShapeDtypeStruct((t, dkv), BF16),
                   jax.ShapeDtypeStruct((dkv, t), BF16)],
        compiler_params=_params("parallel"),
        name="proj_qkv_rope",
    )(x, gain, wqt, wkt, wvt, q_norm, k_norm, cos_t, sin_t)


def _out_proj_kernel(x_ref, o_ref, w_ref, y_ref):
    y_ref[...] = x_ref[...] + jnp.dot(o_ref[...], w_ref[...], preferred_element_type=F32)


def _out_proj(x, o, w):
    t, d = x.shape
    row = pl.BlockSpec((ROW_TILE, d), lambda i: (i, 0))
    return pl.pallas_call(
        _out_proj_kernel,
        grid=(t // ROW_TILE,),
        in_specs=[row, row, _const_spec(w.shape)],
        out_specs=row,
        out_shape=jax.ShapeDtypeStruct((t, d), F32),
        compiler_params=_params("parallel"),
        name="out_proj",
    )(x, o, w)


class _Pipe:
    def __init__(self, k_ref, vt_ref, qm_ref, p0, p1, a0, a1, b0, b1, m_ref, acc_ref, g_ref, tk):
        self.k_ref, self.vt_ref, self.qm_ref = k_ref, vt_ref, qm_ref
        self.p, self.a, self.b = (p0, p1), (a0, a1), (b0, b1)
        self.m_ref, self.acc_ref, self.g_ref, self.tk = m_ref, acc_ref, g_ref, tk

    def reset(self):
        self.m_ref[...] = jnp.full(self.m_ref.shape, NEG_INF, F32)
        self.acc_ref[...] = jnp.zeros(self.acc_ref.shape, F32)

    def _chunks(self):
        n_cols = self.m_ref.shape[1]
        return [slice(j, j + PIPE_CHUNK) for j in range(0, n_cols, PIPE_CHUNK)]

    def _scores(self, k_tile, cs, bias):
        s = jnp.dot(k_tile, self.qm_ref[:, cs], preferred_element_type=F32)
        if bias is not None:
            b0 = cs.start % bias.shape[1]
            s = s + bias[:, b0:b0 + PIPE_CHUNK]
        return s

    def expo(self, slot, tile, bias=None, shift_const=None):
        k0 = pl.multiple_of(tile * self.tk, self.tk)
        k_tile = self.k_ref[pl.ds(k0, self.tk), :]
        jump = None
        for cs in self._chunks():
            s = self._scores(k_tile, cs, bias)
            m_prev = self.m_ref[:, cs]
            m_tile = jnp.max(s, axis=0, keepdims=True)
            if shift_const is not None:
                m_tile = m_tile + shift_const
                self.p[slot][:, cs] = jnp.exp2((s - (m_prev - shift_const)).astype(BF16))
            else:
                self.p[slot][:, cs] = jnp.exp2((s - m_prev).astype(BF16))
            m_new = jnp.maximum(m_prev, m_tile)
            alpha = jnp.exp2(m_prev - m_new)
            self.a[slot][:, cs] = alpha
            self.b[slot][:, cs] = alpha
            self.m_ref[:, cs] = m_new
            gap = jnp.max(m_tile - m_prev)
            jump = gap if jump is None else jnp.maximum(jump, gap)
        self.g_ref[slot] = (jump > GUARD_LOG2).astype(jnp.int32)

    def recompute(self, slot, tile, bias=None, shift_const=None):
        @pl.when(self.g_ref[slot] != 0)
        def _():
            k0 = pl.multiple_of(tile * self.tk, self.tk)
            k_tile = self.k_ref[pl.ds(k0, self.tk), :]
            for cs in self._chunks():
                s = self._scores(k_tile, cs, bias)
                m_new = self.m_ref[:, cs]
                if shift_const is not None:
                    m_new = m_new - shift_const
                self.p[slot][:, cs] = jnp.exp2((s - m_new).astype(BF16))
                self.b[slot][:, cs] = jnp.ones((1, PIPE_CHUNK), F32)
            self.g_ref[slot] = jnp.int32(0)

    def values(self, slot, tile):
        k0 = pl.multiple_of(tile * self.tk, self.tk)
        ones = jnp.ones((BF16_SUBLANES, self.tk), BF16)
        vt = jnp.concatenate([self.vt_ref[:, pl.ds(k0, self.tk)], ones], axis=0)
        for cs in self._chunks():
            pv = jnp.dot(vt, self.p[slot][:, cs], preferred_element_type=F32)
            self.acc_ref[:, cs] = self.a[slot][:, cs] * self.acc_ref[:, cs] + self.b[slot][:, cs] * pv


def _pipe_scratch(tk, n_cols, dv):
    return [pltpu.VMEM((2 * HEAD_DIM, n_cols), BF16),
            pltpu.VMEM((tk, n_cols), BF16), pltpu.VMEM((tk, n_cols), BF16),
            pltpu.VMEM((1, n_cols), F32), pltpu.VMEM((1, n_cols), F32),
            pltpu.VMEM((1, n_cols), F32), pltpu.VMEM((1, n_cols), F32),
            pltpu.VMEM((1, n_cols), F32),
            pltpu.VMEM((dv + BF16_SUBLANES, n_cols), F32),
            pltpu.SMEM((2,), jnp.int32)]


def _attn_a_kernel(lam_ref, far_ref, qt_ref, k_ref, vt_ref, bias_ref, subln_ref, o_ref,
                   qm_ref, p0, p1, a0, a1, b0, b1, m_ref, acc_ref, g_ref, *, seq, tile, out_scale):
    head = pl.program_id(0)
    n_tiles = seq // tile
    n_far = n_tiles - T5_WINDOW
    dv = 2 * HEAD_DIM
    lam = lam_ref[0]
    c_left = far_ref[head, 0]
    c_right = far_ref[head, 1]
    first_half = lax.broadcasted_iota(jnp.int32, (dv, tile), 0) < HEAD_DIM
    pipe = _Pipe(k_ref, vt_ref, qm_ref, p0, p1, a0, a1, b0, b1, m_ref, acc_ref, g_ref, tile)

    def q_body(qi, carry):
        q0 = pl.multiple_of(qi * tile, tile)
        q_tile = qt_ref[:, pl.ds(q0, tile)]
        zeros = jnp.zeros_like(q_tile)
        qm_ref[...] = jnp.concatenate([jnp.where(first_half, q_tile, zeros),
                                       jnp.where(first_half, zeros, q_tile)], axis=1)
        pipe.reset()
        w0 = jnp.clip(qi - 1, 0, n_far)

        def far_tile(f):
            f = jnp.minimum(f, n_far - 1)
            return jnp.where(f < w0, f, f + T5_WINDOW)

        def window_bias(i):
            return bias_ref.at[0, w0 + i - qi + (T5_WINDOW - 1)]

        for i in range(T5_WINDOW):
            if i > 0:
                pipe.recompute((i - 1) % 2, w0 + i - 1, bias=window_bias(i - 1))
                pipe.values((i - 1) % 2, w0 + i - 1)
            pipe.expo(i % 2, w0 + i, bias=window_bias(i))
        last_w = T5_WINDOW - 1
        pipe.recompute(last_w % 2, w0 + last_w, bias=window_bias(last_w))

        def far_step(f, c):
            prev = jnp.where(f == 0, w0 + last_w, far_tile(f - 1))
            prev_shift = jnp.where(f - 1 < w0, c_left, c_right)
            shift = jnp.where(f < w0, c_left, c_right)
            cur = far_tile(f)
            for u in range(2):
                @pl.when((f & 1) == u)
                def _():
                    pipe.recompute(1 - u, prev, shift_const=prev_shift)
                    pipe.expo(u, cur, shift_const=shift)
                    pipe.values(1 - u, prev)
            return c

        if n_far > 0:
            lax.fori_loop(0, n_far, far_step, 0)
            last = far_tile(n_far - 1)
            pipe.recompute(1, last, shift_const=jnp.where(n_far - 1 < w0, c_left, c_right))
            pipe.values(1, last)
        else:
            pipe.values(1, w0 + last_w)

        acc = acc_ref[...]
        acc = acc[:dv] * (1.0 / acc[dv:dv + 1])
        o = acc[:, :tile] - lam * acc[:, tile:]
        ms = jnp.mean(o * o, axis=0, keepdims=True)
        o = o * lax.rsqrt(ms + EPS) * subln_ref[...] * out_scale
        o_ref[pl.ds(q0, tile), :] = o.T.astype(BF16)
        return carry

    lax.fori_loop(0, n_tiles, q_body, 0)


def _attn_a(qt, k, vt, bias_tiles, far_const, lam, subln, batch, seq, out_scale):
    d = k.shape[1]
    hw = 2 * HEAD_DIM
    tile = bias_tiles.shape[-1]
    n_tiles = seq // tile
    assert n_tiles >= T5_WINDOW and n_tiles % 2 == 0
    smem = pl.BlockSpec(memory_space=pltpu.SMEM)
    return pl.pallas_call(
        functools.partial(_attn_a_kernel, seq=seq, tile=tile, out_scale=out_scale),
        grid=(d // hw, batch),
        in_specs=[smem, smem,
                  pl.BlockSpec((hw, seq), lambda h, b: (h, b)),
                  pl.BlockSpec((seq, hw), lambda h, b: (b, h)),
                  pl.BlockSpec((hw, seq), lambda h, b: (h, b)),
                  pl.BlockSpec((1, 2 * T5_WINDOW - 1, tile, tile), lambda h, b: (h, 0, 0, 0),
                               pipeline_mode=pl.Buffered(1)),
                  _const_spec((hw, 1))],
        out_specs=pl.BlockSpec((seq, hw), lambda h, b: (b, h)),
        out_shape=jax.ShapeDtypeStruct(k.shape, BF16),
        scratch_shapes=_pipe_scratch(tile, 2 * tile, hw),
        compiler_params=_params("parallel", "parallel"),
        name="attn_diff",
    )(lam, far_const, qt, k, vt, bias_tiles, subln)


def _t5_bias_vec(t5_table, span):
    nb = T5_BUCKETS // 2
    max_exact = nb // 2
    rel = jnp.arange(-(span - 1), span, dtype=jnp.int32)
    ret = jnp.where(rel > 0, nb, 0)
    n = jnp.abs(rel)
    n_f = jnp.maximum(n, 1).astype(F32)
    large = max_exact + (jnp.log(n_f / max_exact) / math.log(T5_MAX_DIST / max_exact)
                         * (nb - max_exact)).astype(jnp.int32)
    large = jnp.minimum(large, nb - 1)
    bucket = ret + jnp.where(n < max_exact, n, large)
    return t5_table.astype(F32)[bucket]


def _t5_tiles(t5_table, tile):
    assert tile > T5_MAX_DIST
    span = T5_WINDOW * tile
    vec_t = _t5_bias_vec(t5_table, span + 1).T * LOG2E
    period = 2 * tile
    y = np.arange(period)
    shift = np.where(y <= tile, -y, period - y)
    rel = np.stack([j * tile + shift for j in range(1 - T5_WINDOW, T5_WINDOW)])
    gen = vec_t[:, rel + span]
    flat = jnp.tile(gen, (1, 1, tile))[:, :, :tile * (period - 1)]
    tiles = flat.reshape(gen.shape[0], gen.shape[1], tile, period - 1)[:, :, :, :tile]
    far = jnp.stack([vec_t[:, 0], vec_t[:, -1]], axis=1)
    return tiles, far


def _attn_b_kernel(qt_ref, k_ref, vt_ref, o_ref, qm_ref, p0, p1, a0, a1, b0, b1, m_ref, acc_ref, g_ref,
                   *, seq, tq, tk, group):
    upper = pl.program_id(0) % 2 == 1
    n_tiles = seq // tk
    pipe = _Pipe(k_ref, vt_ref, qm_ref, p0, p1, a0, a1, b0, b1, m_ref, acc_ref, g_ref, tk)

    def q_body(qi, carry):
        q0 = pl.multiple_of(qi * tq, tq)
        q_tile = qt_ref[:, pl.ds(q0, tq)]
        q_cat = jnp.concatenate([q_tile[r * HEAD_DIM:(r + 1) * HEAD_DIM] for r in range(group)],
                                axis=1)
        zeros = jnp.zeros_like(q_cat)
        qm_ref[...] = jnp.concatenate([jnp.where(upper, zeros, q_cat),
                                       jnp.where(upper, q_cat, zeros)], axis=0)
        pipe.reset()
        pipe.expo(0, 0)

        def step(t, c):
            for u in range(2):
                @pl.when((t & 1) == u)
                def _():
                    pipe.recompute(1 - u, t - 1)
                    pipe.expo(u, t)
                    pipe.values(1 - u, t - 1)
            return c

        lax.fori_loop(1, n_tiles, step, 0)
        pipe.recompute(1, n_tiles - 1)
        pipe.values(1, n_tiles - 1)
        acc = acc_ref[...]
        o = acc[:HEAD_DIM] * (1.0 / acc[HEAD_DIM:HEAD_DIM + 1])
        o_cat = jnp.concatenate([o[:, r * tq:(r + 1) * tq] for r in range(group)], axis=0)
        o_ref[pl.ds(q0, tq), :] = o_cat.T.astype(BF16)
        return carry

    lax.fori_loop(0, seq // tq, q_body, 0)


def _attn_b(qt, k, vt, batch, seq):
    dq = qt.shape[0]
    n_kv = vt.shape[0] // HEAD_DIM
    group = dq // HEAD_DIM // n_kv
    gw = group * HEAD_DIM
    tq, tk = ATTN_B_TQ, ATTN_B_TK
    assert seq % (2 * tk) == 0 and seq % tq == 0
    return pl.pallas_call(
        functools.partial(_attn_b_kernel, seq=seq, tq=tq, tk=tk, group=group),
        grid=(n_kv, batch),
        in_specs=[pl.BlockSpec((gw, seq), lambda g, b: (g, b)),
                  pl.BlockSpec((seq, 2 * HEAD_DIM), lambda g, b: (b, g // 2)),
                  pl.BlockSpec((HEAD_DIM, seq), lambda g, b: (g, b))],
        out_specs=pl.BlockSpec((seq, gw), lambda g, b: (b, g)),
        out_shape=jax.ShapeDtypeStruct((batch * seq, dq), BF16),
        scratch_shapes=_pipe_scratch(tk, group * tq, HEAD_DIM),
        compiler_params=_params("parallel", "parallel"),
        name="attn_gqa",
    )(qt, k, vt)


def _rope_tables(seq):
    n = HEAD_DIM // 4
    t = jnp.arange(seq, dtype=jnp.int32)
    freqs = ROPE_THETA ** (-jnp.arange(n, dtype=F32) / n)

    def half(pos):
        ang = pos.astype(F32)[None, :] * freqs[:, None]
        c, s = jnp.cos(ang), jnp.sin(ang)
        return jnp.concatenate([c, c], axis=0), jnp.concatenate([-s, s], axis=0)

    c_row, s_row = half(t // GRID_W)
    c_col, s_col = half(t % GRID_W)
    return jnp.concatenate([c_row, c_col], axis=0), jnp.concatenate([s_row, s_col], axis=0)


def _attn_c_kernel(qt_ref, k_ref, vt_ref, bm_ref, o_ref, *, rows):
    j = pl.program_id(2)
    n_keys = NA_K_ROWS * GRID_W
    w0 = jnp.clip(j * NA_Q_ROWS - NA_ROWS // 2, 0, rows - NA_K_ROWS)
    k0 = pl.multiple_of(w0 * GRID_W, NA_ROWS // 2 * GRID_W)
    k_win = k_ref[pl.ds(k0, n_keys), :]
    q_tile = qt_ref[...]
    zeros = jnp.zeros_like(q_tile)
    first_head = lax.broadcasted_iota(jnp.int32, q_tile.shape, 0) < HEAD_DIM
    ones = jnp.ones((BF16_SUBLANES, n_keys), BF16)
    outs = []
    for hh in range(2):
        q_pad = jnp.where(first_head, q_tile, zeros) if hh == 0 else jnp.where(first_head, zeros, q_tile)
        s = jnp.dot(k_win, q_pad, preferred_element_type=F32) + bm_ref[hh, 0]
        p = jnp.exp2((s - jnp.max(s, axis=0, keepdims=True)).astype(BF16))
        v_win = jnp.concatenate([vt_ref[hh * HEAD_DIM:(hh + 1) * HEAD_DIM, pl.ds(k0, n_keys)], ones],
                                axis=0)
        acc = jnp.dot(v_win, p, preferred_element_type=F32)
        outs.append(acc[:HEAD_DIM] * (1.0 / acc[HEAD_DIM:HEAD_DIM + 1]))
    o_ref[...] = jnp.concatenate(outs, axis=0).T.astype(BF16)


def _attn_c(qt, k, vt, biasmask, batch, seq):
    d = k.shape[1]
    hw = 2 * HEAD_DIM
    rows = seq // GRID_W
    assert rows >= NA_K_ROWS and rows % NA_Q_ROWS == 0
    n_blk = rows // NA_Q_ROWS
    n_q = NA_Q_ROWS * GRID_W
    n_keys = NA_K_ROWS * GRID_W

    def variant(j):
        return jnp.where(j == 0, 0, jnp.where(j == n_blk - 1, 2, 1))

    return pl.pallas_call(
        functools.partial(_attn_c_kernel, rows=rows),
        grid=(d // hw, batch, n_blk),
        in_specs=[pl.BlockSpec((hw, n_q), lambda h, b, j: (h, b * n_blk + j)),
                  pl.BlockSpec((seq, hw), lambda h, b, j: (b, h)),
                  pl.BlockSpec((hw, seq), lambda h, b, j: (h, b)),
                  pl.BlockSpec((2, 1, n_keys, n_q), lambda h, b, j: (h, variant(j), 0, 0))],
        out_specs=pl.BlockSpec((n_q, hw), lambda h, b, j: (b * n_blk + j, h)),
        out_shape=jax.ShapeDtypeStruct(k.shape, BF16),
        compiler_params=_params("parallel", "parallel", "arbitrary"),
        name="attn_nbr",
    )(qt, k, vt, biasmask)


def _na_biasmask(rpb):
    rows = 4 * NA_Q_ROWS
    n_heads = rpb.shape[0]
    col = np.arange(GRID_W)
    c_start = np.clip(col - NA_COLS // 2, 0, GRID_W - NA_COLS)
    col_ok = (col[:, None] >= c_start[None, :]) & (col[:, None] < c_start[None, :] + NA_COLS)
    c_idx = np.clip(col[:, None] - col[None, :] + NA_COLS - 1, 0, 2 * NA_COLS - 2)
    by_col = jnp.take(rpb.astype(F32) * LOG2E, c_idx.reshape(-1), axis=2)
    n_blk = rows // NA_Q_ROWS
    tiles = []
    for j in (0, 1, n_blk - 1):
        w0 = min(max(j * NA_Q_ROWS - NA_ROWS // 2, 0), rows - NA_K_ROWS)
        kr = w0 + np.arange(NA_K_ROWS)
        qr = j * NA_Q_ROWS + np.arange(NA_Q_ROWS)
        r_start = np.clip(qr - NA_ROWS // 2, 0, rows - NA_ROWS)
        row_ok = (kr[:, None] >= r_start[None, :]) & (kr[:, None] < r_start[None, :] + NA_ROWS)
        r_idx = np.clip(kr[:, None] - qr[None, :] + NA_ROWS - 1, 0, 2 * NA_ROWS - 2)
        bias = jnp.take(by_col, r_idx.reshape(-1), axis=1)
        bias = bias.reshape(n_heads, NA_K_ROWS, NA_Q_ROWS, GRID_W, GRID_W)
        ok = row_ok[:, :, None, None] & col_ok[None, None, :, :]
        bias = jnp.where(ok[None], bias, NEG_INF)
        tiles.append(jnp.transpose(bias, (0, 1, 3, 2, 4)).reshape(
            n_heads, NA_K_ROWS * GRID_W, NA_Q_ROWS * GRID_W))
    return jnp.stack(tiles, axis=1)


def _trunk(x3, w):
    batch, seq, d = x3.shape
    x = x3.reshape(batch * seq, d)
    depth = w["ffn_w_in"].shape[0]
    for i in range(depth):
        j, kind = divmod(i, N_MIXERS)
        x = _ffn(x, w["ffn_norm"][i, 0], w["ffn_w_in"][i, 0], w["ffn_w_out"][i, 0], w["final_norm"], False)
        if kind == 0:
            lambda_init = 0.8 - 0.6 * math.exp(-0.3 * i)
            qt, k, vt = _proj(x, w["mix_norm"][i], *w["a_qkv"][j])
            lamf = w["a_lambda"][j].astype(F32)
            lam = (jnp.exp(jnp.sum(lamf[0] * lamf[1])) - jnp.exp(jnp.sum(lamf[2] * lamf[3]))
                   + lambda_init).reshape(1)
            o = _attn_a(qt, k, vt, *w["t5"], lam, w["a_subln"][j], batch, seq, 1.0 - lambda_init)
            w_out = w["a_w_out"][j]
        elif kind == 1:
            cos_t, sin_t = _rope_tables(seq)
            qt, k, vt = _proj_b(x, w["mix_norm"][i], *w["b_qkv"][j], w["b_q_norm"][j], w["b_k_norm"][j],
                                cos_t, sin_t)
            o = _attn_b(qt, k, vt, batch, seq)
            w_out = w["b_w_out"][j]
        else:
            qt, k, vt = _proj(x, w["mix_norm"][i], *w["c_qkv"][j])
            o = _attn_c(qt, k, vt, w["c_biasmask"][j], batch, seq)
            w_out = w["c_w_out"][j]
        x = _out_proj(x, o, w_out)
        x = _ffn(x, w["ffn_norm"][i, 1], w["ffn_w_in"][i, 1], w["ffn_w_out"][i, 1], w["final_norm"],
                 i == depth - 1)
    return x.reshape(batch, seq, d)


def _split_qkv_t(w_qkv, n_q, n_k):
    wq = jnp.swapaxes(w_qkv[:, :, :n_q], 1, 2).astype(BF16)
    wk = w_qkv[:, :, n_q:n_q + n_k].astype(BF16)
    wv = jnp.swapaxes(w_qkv[:, :, n_q + n_k:], 1, 2).astype(BF16)
    return [(wq[j], wk[j], wv[j]) for j in range(w_qkv.shape[0])]


def kernel(x_prompt, x_sample, ffn_norm, ffn_w_in, ffn_w_out, mix_norm, a_w_qkv, a_lambda, a_subln, a_w_out,
           t5_table, b_w_qkv, b_q_norm, b_k_norm, b_w_out, c_w_qkv, c_rpb, c_w_out, final_norm):
    d = x_prompt.shape[-1]
    n_kv = (b_w_qkv.shape[-1] - d) // 2
    b_qkv = _split_qkv_t(b_w_qkv, d, n_kv)
    w = {
        "ffn_norm": ffn_norm.astype(F32)[:, :, None, :],
        "ffn_w_in": ffn_w_in.astype(BF16),
        "ffn_w_out": ffn_w_out.astype(BF16),
        "mix_norm": mix_norm.astype(F32)[:, None, :],
        "final_norm": final_norm.astype(F32)[None, :],
        "a_qkv": _split_qkv_t(a_w_qkv, d, d),
        "a_lambda": a_lambda,
        "a_subln": a_subln.astype(F32)[:, :, None],
        "a_w_out": a_w_out.astype(BF16),
        "t5": _t5_tiles(t5_table, ATTN_A_TILE),
        "b_qkv": [(wq, jnp.swapaxes(wk, 0, 1), wv) for wq, wk, wv in b_qkv],
        "b_q_norm": b_q_norm.astype(F32)[:, :, None],
        "b_k_norm": b_k_norm.astype(F32)[:, :, None],
        "b_w_out": b_w_out.astype(BF16),
        "c_qkv": _split_qkv_t(c_w_qkv, d, d),
        "c_biasmask": [_na_biasmask(c_rpb[j]) for j in range(c_rpb.shape[0])],
        "c_w_out": c_w_out.astype(BF16),
    }
    return (_trunk(x_prompt, w), _trunk(x_sample, w))
```

```python
import functools
import math

import numpy as np
import jax
import jax.numpy as jnp
from jax import lax
from jax.experimental import pallas as pl
from jax.experimental.pallas import tpu as pltpu

F32 = jnp.float32
BF16 = jnp.bfloat16

EPS = 1e-6
HEAD_DIM = 64
GRID_W = 64
NEG_INF = -1e30
N_MIXERS = 3
T5_BUCKETS = 32
T5_MAX_DIST = 128
ROPE_THETA = 10000.0
NA_ROWS = 8
NA_COLS = 16
LOG2E = math.log2(math.e)
Q_SCALE = HEAD_DIM ** -0.5 * LOG2E

V7X_VMEM_BYTES = 64 * 1024 * 1024
VMEM_LIMIT_BYTES = V7X_VMEM_BYTES - 8 * 1024 * 1024
MXU_DIM = 256
BF16_SUBLANES = 16
ROW_TILE = 512
FFN_CHUNK = MXU_DIM
ATTN_A_TILE = 512
PIPE_CHUNK = 256
GUARD_LOG2 = 64.0
T5_WINDOW = 4
ATTN_B_TQ = 256
ATTN_B_TK = 1024
NA_Q_ROWS = 8
NA_K_ROWS = 16


def _params(*sem):
    return pltpu.CompilerParams(dimension_semantics=sem, vmem_limit_bytes=VMEM_LIMIT_BYTES)


def _const_spec(shape):
    nd = len(shape)
    return pl.BlockSpec(shape, lambda *_: (0,) * nd, pipeline_mode=pl.Buffered(1))


def _rms(x, gain):
    ms = jnp.mean(x * x, axis=-1, keepdims=True)
    return x * lax.rsqrt(ms + EPS) * gain


def _ffn_kernel(x_ref, g_ref, wi_ref, wo_ref, fg_ref, o_ref, act_ref, *, d_ff, final):
    x = x_ref[...]
    xn = _rms(x, g_ref[...]).astype(BF16)
    for c in range(d_ff // FFN_CHUNK):
        lo = c * FFN_CHUNK
        gate = jnp.dot(xn, wi_ref[:, lo:lo + FFN_CHUNK], preferred_element_type=F32)
        up = jnp.dot(xn, wi_ref[:, d_ff + lo:d_ff + lo + FFN_CHUNK], preferred_element_type=F32)
        act_ref[:, lo:lo + FFN_CHUNK] = (gate * jax.nn.sigmoid(gate) * up).astype(BF16)
    y = x + 0.5 * jnp.dot(act_ref[...], wo_ref[...], preferred_element_type=F32)
    if final:
        y = _rms(y, fg_ref[...])
    o_ref[...] = y


def _ffn(x, gain, w_in, w_out, final_gain, final):
    t, d = x.shape
    d_ff = w_out.shape[0]
    row = pl.BlockSpec((ROW_TILE, d), lambda i: (i, 0))
    return pl.pallas_call(
        functools.partial(_ffn_kernel, d_ff=d_ff, final=final),
        grid=(t // ROW_TILE,),
        in_specs=[row, _const_spec((1, d)), _const_spec(w_in.shape), _const_spec(w_out.shape),
                  _const_spec((1, d))],
        out_specs=row,
        out_shape=jax.ShapeDtypeStruct((t, d), F32),
        scratch_shapes=[pltpu.VMEM((ROW_TILE, d_ff), BF16)],
        compiler_params=_params("parallel"),
        name="ffn",
    )(x, gain, w_in, w_out, final_gain)


def _dot_nt(a, b):
    return lax.dot_general(a, b, (((1,), (1,)), ((), ())), preferred_element_type=F32)


def _proj_kernel(x_ref, g_ref, wqt_ref, wk_ref, wvt_ref, qt_ref, k_ref, vt_ref):
    hn = _rms(x_ref[...], g_ref[...]).astype(BF16)
    qt_ref[...] = (_dot_nt(wqt_ref[...], hn) * Q_SCALE).astype(BF16)
    k_ref[...] = jnp.dot(hn, wk_ref[...], preferred_element_type=F32).astype(BF16)
    vt_ref[...] = _dot_nt(wvt_ref[...], hn).astype(BF16)


def _proj(x, gain, wqt, wk, wvt):
    t, d = x.shape
    row = pl.BlockSpec((ROW_TILE, d), lambda i: (i, 0))
    col = pl.BlockSpec((d, ROW_TILE), lambda i: (0, i))
    return pl.pallas_call(
        _proj_kernel,
        grid=(t // ROW_TILE,),
        in_specs=[row, _const_spec((1, d)), _const_spec(wqt.shape), _const_spec(wk.shape),
                  _const_spec(wvt.shape)],
        out_specs=[col, row, col],
        out_shape=[jax.ShapeDtypeStruct((d, t), BF16), jax.ShapeDtypeStruct((t, d), BF16),
                   jax.ShapeDtypeStruct((d, t), BF16)],
        compiler_params=_params("parallel"),
        name="proj_qkv",
    )(x, gain, wqt, wk, wvt)


def _norm_rope_t(yt, gain, cos, sin, n_heads):
    q4 = HEAD_DIM // 4
    out = []
    for h in range(n_heads):
        y = yt[h * HEAD_DIM:(h + 1) * HEAD_DIM]
        ms = jnp.mean(y * y, axis=0, keepdims=True)
        yn = y * lax.rsqrt(ms + EPS) * gain
        swapped = jnp.concatenate([yn[q4:2 * q4], yn[:q4], yn[3 * q4:], yn[2 * q4:3 * q4]], axis=0)
        out.append(yn * cos + swapped * sin)
    return jnp.concatenate(out, axis=0)


def _proj_b_kernel(x_ref, g_ref, wqt_ref, wkt_ref, wvt_ref, qn_ref, kn_ref, cos_ref, sin_ref,
                   qt_ref, k_ref, vt_ref, *, n_q, n_kv):
    hn = _rms(x_ref[...], g_ref[...]).astype(BF16)
    cos = cos_ref[...]
    sin = sin_ref[...]
    q = _norm_rope_t(_dot_nt(wqt_ref[...], hn), qn_ref[...], cos, sin, n_q)
    qt_ref[...] = (q * Q_SCALE).astype(BF16)
    k = _norm_rope_t(_dot_nt(wkt_ref[...], hn), kn_ref[...], cos, sin, n_kv)
    k_ref[...] = k.T.astype(BF16)
    vt_ref[...] = _dot_nt(wvt_ref[...], hn).astype(BF16)


def _proj_b(x, gain, wqt, wkt, wvt, q_norm, k_norm, cos_t, sin_t):
    t, d = x.shape
    dq, dkv = wqt.shape[0], wkt.shape[0]
    s_tiles = cos_t.shape[1] // ROW_TILE
    row = pl.BlockSpec((ROW_TILE, d), lambda i: (i, 0))
    tab = pl.BlockSpec((HEAD_DIM, ROW_TILE), lambda i: (0, i % s_tiles))
    return pl.pallas_call(
        functools.partial(_proj_b_kernel, n_q=dq // HEAD_DIM, n_kv=dkv // HEAD_DIM),
        grid=(t // ROW_TILE,),
        in_specs=[row, _const_spec((1, d)), _const_spec(wqt.shape), _const_spec(wkt.shape),
                  _const_spec(wvt.shape), _const_spec((HEAD_DIM, 1)), _const_spec((HEAD_DIM, 1)),
                  tab, tab],
        out_specs=[pl.BlockSpec((dq, ROW_TILE), lambda i: (0, i)),
                   pl.BlockSpec((ROW_TILE, dkv), lambda i: (i, 0)),
                   pl.BlockSpec((dkv, ROW_TILE), lambda i: (0, i))],
        out_shape=[jax.ShapeDtypeStruct((dq, t), BF16), jax.ShapeDtypeStruct((t, dkv), BF16),
                   jax.ShapeDtypeStruct((dkv, t), BF16)],
        compiler_params=_params("parallel"),
        name="proj_qkv_rope",
    )(x, gain, wqt, wkt, wvt, q_norm, k_norm, cos_t, sin_t)


def _out_proj_kernel(x_ref, o_ref, w_ref, y_ref):
    y_ref[...] = x_ref[...] + jnp.dot(o_ref[...], w_ref[...], preferred_element_type=F32)


def _out_proj(x, o, w):
    t, d = x.shape
    row = pl.BlockSpec((ROW_TILE, d), lambda i: (i, 0))
    return pl.pallas_call(
        _out_proj_kernel,
        grid=(t // ROW_TILE,),
        in_specs=[row, row, _const_spec(w.shape)],
        out_specs=row,
        out_shape=jax.ShapeDtypeStruct((t, d), F32),
        compiler_params=_params("parallel"),
        name="out_proj",
    )(x, o, w)


class _Pipe:
    def __init__(self, k_ref, vt_ref, qm_ref, p0, p1, a0, a1, b0, b1, m_ref, acc_ref, g_ref, tk):
        self.k_ref, self.vt_ref, self.qm_ref = k_ref, vt_ref, qm_ref
        self.p, self.a, self.b = (p0, p1), (a0, a1), (b0, b1)
        self.m_ref, self.acc_ref, self.g_ref, self.tk = m_ref, acc_ref, g_ref, tk

    def reset(self):
        self.m_ref[...] = jnp.full(self.m_ref.shape, NEG_INF, F32)
        self.acc_ref[...] = jnp.zeros(self.acc_ref.shape, F32)

    def _chunks(self):
        n_cols = self.m_ref.shape[1]
        return [slice(j, j + PIPE_CHUNK) for j in range(0, n_cols, PIPE_CHUNK)]

    def _scores(self, k_tile, cs, bias):
        s = jnp.dot(k_tile, self.qm_ref[:, cs], preferred_element_type=F32)
        if bias is not None:
            b0 = cs.start % bias.shape[1]
            s = s + bias[:, b0:b0 + PIPE_CHUNK]
        return s

    def expo(self, slot, tile, bias=None, shift_const=None):
        k0 = pl.multiple_of(tile * self.tk, self.tk)
        k_tile = self.k_ref[pl.ds(k0, self.tk), :]
        jump = None
        for cs in self._chunks():
            s = self._scores(k_tile, cs, bias)
            m_prev = self.m_ref[:, cs]
            m_tile = jnp.max(s, axis=0, keepdims=True)
            if shift_const is not None:
                m_tile = m_tile + shift_const
                self.p[slot][:, cs] = jnp.exp2(s - (m_prev - shift_const)).astype(BF16)
            else:
                self.p[slot][:, cs] = jnp.exp2(s - m_prev).astype(BF16)
            m_new = jnp.maximum(m_prev, m_tile)
            alpha = jnp.exp2(m_prev - m_new)
            self.a[slot][:, cs] = alpha
            self.b[slot][:, cs] = alpha
            self.m_ref[:, cs] = m_new
            gap = jnp.max(m_tile - m_prev)
            jump = gap if jump is None else jnp.maximum(jump, gap)
        self.g_ref[slot] = (jump > GUARD_LOG2).astype(jnp.int32)

    def recompute(self, slot, tile, bias=None, shift_const=None):
        @pl.when(self.g_ref[slot] != 0)
        def _():
            k0 = pl.multiple_of(tile * self.tk, self.tk)
            k_tile = self.k_ref[pl.ds(k0, self.tk), :]
            for cs in self._chunks():
                s = self._scores(k_tile, cs, bias)
                m_new = self.m_ref[:, cs]
                if shift_const is not None:
                    m_new = m_new - shift_const
                self.p[slot][:, cs] = jnp.exp2(s - m_new).astype(BF16)
                self.b[slot][:, cs] = jnp.ones((1, PIPE_CHUNK), F32)
            self.g_ref[slot] = jnp.int32(0)

    def values(self, slot, tile):
        k0 = pl.multiple_of(tile * self.tk, self.tk)
        ones = jnp.ones((BF16_SUBLANES, self.tk), BF16)
        vt = jnp.concatenate([self.vt_ref[:, pl.ds(k0, self.tk)], ones], axis=0)
        for cs in self._chunks():
            pv = jnp.dot(vt, self.p[slot][:, cs], preferred_element_type=F32)
            self.acc_ref[:, cs] = self.a[slot][:, cs] * self.acc_ref[:, cs] + self.b[slot][:, cs] * pv


def _pipe_scratch(tk, n_cols, dv):
    return [pltpu.VMEM((2 * HEAD_DIM, n_cols), BF16),
            pltpu.VMEM((tk, n_cols), BF16), pltpu.VMEM((tk, n_cols), BF16),
            pltpu.VMEM((1, n_cols), F32), pltpu.VMEM((1, n_cols), F32),
            pltpu.VMEM((1, n_cols), F32), pltpu.VMEM((1, n_cols), F32),
            pltpu.VMEM((1, n_cols), F32),
            pltpu.VMEM((dv + BF16_SUBLANES, n_cols), F32),
            pltpu.SMEM((2,), jnp.int32)]


def _attn_a_kernel(lam_ref, far_ref, qt_ref, k_ref, vt_ref, bias_ref, subln_ref, o_ref,
                   qm_ref, p0, p1, a0, a1, b0, b1, m_ref, acc_ref, g_ref, *, seq, tile, out_scale):
    head = pl.program_id(0)
    n_tiles = seq // tile
    n_far = n_tiles - T5_WINDOW
    dv = 2 * HEAD_DIM
    lam = lam_ref[0]
    c_left = far_ref[head, 0]
    c_right = far_ref[head, 1]
    first_half = lax.broadcasted_iota(jnp.int32, (dv, tile), 0) < HEAD_DIM
    pipe = _Pipe(k_ref, vt_ref, qm_ref, p0, p1, a0, a1, b0, b1, m_ref, acc_ref, g_ref, tile)

    def q_body(qi, carry):
        q0 = pl.multiple_of(qi * tile, tile)
        q_tile = qt_ref[:, pl.ds(q0, tile)]
        zeros = jnp.zeros_like(q_tile)
        qm_ref[...] = jnp.concatenate([jnp.where(first_half, q_tile, zeros),
                                       jnp.where(first_half, zeros, q_tile)], axis=1)
        pipe.reset()
        w0 = jnp.clip(qi - 1, 0, n_far)

        def far_tile(f):
            f = jnp.minimum(f, n_far - 1)
            return jnp.where(f < w0, f, f + T5_WINDOW)

        def window_bias(i):
            return bias_ref.at[0, w0 + i - qi + (T5_WINDOW - 1)]

        for i in range(T5_WINDOW):
            if i > 0:
                pipe.recompute((i - 1) % 2, w0 + i - 1, bias=window_bias(i - 1))
            pipe.expo(i % 2, w0 + i, bias=window_bias(i))
            if i > 0:
                pipe.values((i - 1) % 2, w0 + i - 1)
        last_w = T5_WINDOW - 1
        pipe.recompute(last_w % 2, w0 + last_w, bias=window_bias(last_w))

        def far_step(f, c):
            prev = jnp.where(f == 0, w0 + last_w, far_tile(f - 1))
            prev_shift = jnp.where(f - 1 < w0, c_left, c_right)
            shift = jnp.where(f < w0, c_left, c_right)
            cur = far_tile(f)
            for u in range(2):
                @pl.when((f & 1) == u)
                def _():
                    pipe.recompute(1 - u, prev, shift_const=prev_shift)
                    pipe.expo(u, cur, shift_const=shift)
                    pipe.values(1 - u, prev)
            return c

        if n_far > 0:
            lax.fori_loop(0, n_far, far_step, 0)
            last = far_tile(n_far - 1)
            pipe.recompute(1, last, shift_const=jnp.where(n_far - 1 < w0, c_left, c_right))
            pipe.values(1, last)
        else:
            pipe.values(1, w0 + last_w)

        acc = acc_ref[...]
        acc = acc[:dv] * (1.0 / acc[dv:dv + 1])
        o = acc[:, :tile] - lam * acc[:, tile:]
        ms = jnp.mean(o * o, axis=0, keepdims=True)
        o = o * lax.rsqrt(ms + EPS) * subln_ref[...] * out_scale
        o_ref[pl.ds(q0, tile), :] = o.T.astype(BF16)
        return carry

    lax.fori_loop(0, n_tiles, q_body, 0)


def _attn_a(qt, k, vt, bias_tiles, far_const, lam, subln, batch, seq, out_scale):
    d = k.shape[1]
    hw = 2 * HEAD_DIM
    tile = bias_tiles.shape[-1]
    n_tiles = seq // tile
    assert n_tiles >= T5_WINDOW and n_tiles % 2 == 0
    smem = pl.BlockSpec(memory_space=pltpu.SMEM)
    return pl.pallas_call(
        functools.partial(_attn_a_kernel, seq=seq, tile=tile, out_scale=out_scale),
        grid=(d // hw, batch),
        in_specs=[smem, smem,
                  pl.BlockSpec((hw, seq), lambda h, b: (h, b)),
                  pl.BlockSpec((seq, hw), lambda h, b: (b, h)),
                  pl.BlockSpec((hw, seq), lambda h, b: (h, b)),
                  pl.BlockSpec((1, 2 * T5_WINDOW - 1, tile, tile), lambda h, b: (h, 0, 0, 0),
                               pipeline_mode=pl.Buffered(1)),
                  _const_spec((hw, 1))],
        out_specs=pl.BlockSpec((seq, hw), lambda h, b: (b, h)),
        out_shape=jax.ShapeDtypeStruct(k.shape, BF16),
        scratch_shapes=_pipe_scratch(tile, 2 * tile, hw),
        compiler_params=_params("parallel", "parallel"),
        name="attn_diff",
    )(lam, far_const, qt, k, vt, bias_tiles, subln)


def _t5_bias_vec(t5_table, span):
    nb = T5_BUCKETS // 2
    max_exact = nb // 2
    rel = jnp.arange(-(span - 1), span, dtype=jnp.int32)
    ret = jnp.where(rel > 0, nb, 0)
    n = jnp.abs(rel)
    n_f = jnp.maximum(n, 1).astype(F32)
    large = max_exact + (jnp.log(n_f / max_exact) / math.log(T5_MAX_DIST / max_exact)
                         * (nb - max_exact)).astype(jnp.int32)
    large = jnp.minimum(large, nb - 1)
    bucket = ret + jnp.where(n < max_exact, n, large)
    return t5_table.astype(F32)[bucket]


def _t5_tiles(t5_table, tile):
    assert tile > T5_MAX_DIST
    span = T5_WINDOW * tile
    vec_t = _t5_bias_vec(t5_table, span + 1).T * LOG2E
    period = 2 * tile
    y = np.arange(period)
    shift = np.where(y <= tile, -y, period - y)
    rel = np.stack([j * tile + shift for j in range(1 - T5_WINDOW, T5_WINDOW)])
    gen = vec_t[:, rel + span]
    flat = jnp.tile(gen, (1, 1, tile))[:, :, :tile * (period - 1)]
    tiles = flat.reshape(gen.shape[0], gen.shape[1], tile, period - 1)[:, :, :, :tile]
    far = jnp.stack([vec_t[:, 0], vec_t[:, -1]], axis=1)
    return tiles, far


def _attn_b_kernel(qt_ref, k_ref, vt_ref, o_ref, qm_ref, p0, p1, a0, a1, b0, b1, m_ref, acc_ref, g_ref,
                   *, seq, tq, tk, group):
    upper = pl.program_id(0) % 2 == 1
    n_tiles = seq // tk
    pipe = _Pipe(k_ref, vt_ref, qm_ref, p0, p1, a0, a1, b0, b1, m_ref, acc_ref, g_ref, tk)

    def q_body(qi, carry):
        q0 = pl.multiple_of(qi * tq, tq)
        q_tile = qt_ref[:, pl.ds(q0, tq)]
        q_cat = jnp.concatenate([q_tile[r * HEAD_DIM:(r + 1) * HEAD_DIM] for r in range(group)],
                                axis=1)
        zeros = jnp.zeros_like(q_cat)
        qm_ref[...] = jnp.concatenate([jnp.where(upper, zeros, q_cat),
                                       jnp.where(upper, q_cat, zeros)], axis=0)
        pipe.reset()
        pipe.expo(0, 0)

        def step(t, c):
            for u in range(2):
                @pl.when((t & 1) == u)
                def _():
                    pipe.recompute(1 - u, t - 1)
                    pipe.expo(u, t)
                    pipe.values(1 - u, t - 1)
            return c

        lax.fori_loop(1, n_tiles, step, 0)
        pipe.recompute(1, n_tiles - 1)
        pipe.values(1, n_tiles - 1)
        acc = acc_ref[...]
        o = acc[:HEAD_DIM] * (1.0 / acc[HEAD_DIM:HEAD_DIM + 1])
        o_cat = jnp.concatenate([o[:, r * tq:(r + 1) * tq] for r in range(group)], axis=0)
        o_ref[pl.ds(q0, tq), :] = o_cat.T.astype(BF16)
        return carry

    lax.fori_loop(0, seq // tq, q_body, 0)


def _attn_b(qt, k, vt, batch, seq):
    dq = qt.shape[0]
    n_kv = vt.shape[0] // HEAD_DIM
    group = dq // HEAD_DIM // n_kv
    gw = group * HEAD_DIM
    tq, tk = ATTN_B_TQ, ATTN_B_TK
    assert seq % (2 * tk) == 0 and seq % tq == 0
    return pl.pallas_call(
        functools.partial(_attn_b_kernel, seq=seq, tq=tq, tk=tk, group=group),
        grid=(n_kv, batch),
        in_specs=[pl.BlockSpec((gw, seq), lambda g, b: (g, b)),
                  pl.BlockSpec((seq, 2 * HEAD_DIM), lambda g, b: (b, g // 2)),
                  pl.BlockSpec((HEAD_DIM, seq), lambda g, b: (g, b))],
        out_specs=pl.BlockSpec((seq, gw), lambda g, b: (b, g)),
        out_shape=jax.ShapeDtypeStruct((batch * seq, dq), BF16),
        scratch_shapes=_pipe_scratch(tk, group * tq, HEAD_DIM),
        compiler_params=_params("parallel", "parallel"),
        name="attn_gqa",
    )(qt, k, vt)


def _rope_tables(seq):
    n = HEAD_DIM // 4
    t = jnp.arange(seq, dtype=jnp.int32)
    freqs = ROPE_THETA ** (-jnp.arange(n, dtype=F32) / n)

    def half(pos):
        ang = pos.astype(F32)[None, :] * freqs[:, None]
        c, s = jnp.cos(ang), jnp.sin(ang)
        return jnp.concatenate([c, c], axis=0), jnp.concatenate([-s, s], axis=0)

    c_row, s_row = half(t // GRID_W)
    c_col, s_col = half(t % GRID_W)
    return jnp.concatenate([c_row, c_col], axis=0), jnp.concatenate([s_row, s_col], axis=0)


def _attn_c_kernel(qt_ref, k_ref, vt_ref, bm_ref, o_ref, *, rows):
    j = pl.program_id(2)
    n_keys = NA_K_ROWS * GRID_W
    w0 = jnp.clip(j * NA_Q_ROWS - NA_ROWS // 2, 0, rows - NA_K_ROWS)
    k0 = pl.multiple_of(w0 * GRID_W, NA_ROWS // 2 * GRID_W)
    k_win = k_ref[pl.ds(k0, n_keys), :]
    q_tile = qt_ref[...]
    zeros = jnp.zeros_like(q_tile)
    first_head = lax.broadcasted_iota(jnp.int32, q_tile.shape, 0) < HEAD_DIM
    ones = jnp.ones((BF16_SUBLANES, n_keys), BF16)
    outs = []
    for hh in range(2):
        q_pad = jnp.where(first_head, q_tile, zeros) if hh == 0 else jnp.where(first_head, zeros, q_tile)
        s = jnp.dot(k_win, q_pad, preferred_element_type=F32) + bm_ref[hh, 0]
        p = jnp.exp2(s - jnp.max(s, axis=0, keepdims=True)).astype(BF16)
        v_win = jnp.concatenate([vt_ref[hh * HEAD_DIM:(hh + 1) * HEAD_DIM, pl.ds(k0, n_keys)], ones],
                                axis=0)
        acc = jnp.dot(v_win, p, preferred_element_type=F32)
        outs.append(acc[:HEAD_DIM] * (1.0 / acc[HEAD_DIM:HEAD_DIM + 1]))
    o_ref[...] = jnp.concatenate(outs, axis=0).T.astype(BF16)


def _attn_c(qt, k, vt, biasmask, batch, seq):
    d = k.shape[1]
    hw = 2 * HEAD_DIM
    rows = seq // GRID_W
    assert rows >= NA_K_ROWS and rows % NA_Q_ROWS == 0
    n_blk = rows // NA_Q_ROWS
    n_q = NA_Q_ROWS * GRID_W
    n_keys = NA_K_ROWS * GRID_W

    def variant(j):
        return jnp.where(j == 0, 0, jnp.where(j == n_blk - 1, 2, 1))

    return pl.pallas_call(
        functools.partial(_attn_c_kernel, rows=rows),
        grid=(d // hw, batch, n_blk),
        in_specs=[pl.BlockSpec((hw, n_q), lambda h, b, j: (h, b * n_blk + j)),
                  pl.BlockSpec((seq, hw), lambda h, b, j: (b, h)),
                  pl.BlockSpec((hw, seq), lambda h, b, j: (h, b)),
                  pl.BlockSpec((2, 1, n_keys, n_q), lambda h, b, j: (h, variant(j), 0, 0))],
        out_specs=pl.BlockSpec((n_q, hw), lambda h, b, j: (b * n_blk + j, h)),
        out_shape=jax.ShapeDtypeStruct(k.shape, BF16),
        compiler_params=_params("parallel", "parallel", "arbitrary"),
        name="attn_nbr",
    )(qt, k, vt, biasmask)


def _na_biasmask(rpb):
    rows = 4 * NA_Q_ROWS
    n_heads = rpb.shape[0]
    col = np.arange(GRID_W)
    c_start = np.clip(col - NA_COLS // 2, 0, GRID_W - NA_COLS)
    col_ok = (col[:, None] >= c_start[None, :]) & (col[:, None] < c_start[None, :] + NA_COLS)
    c_idx = np.clip(col[:, None] - col[None, :] + NA_COLS - 1, 0, 2 * NA_COLS - 2)
    by_col = jnp.take(rpb.astype(F32) * LOG2E, c_idx.reshape(-1), axis=2)
    n_blk = rows // NA_Q_ROWS
    tiles = []
    for j in (0, 1, n_blk - 1):
        w0 = min(max(j * NA_Q_ROWS - NA_ROWS // 2, 0), rows - NA_K_ROWS)
        kr = w0 + np.arange(NA_K_ROWS)
        qr = j * NA_Q_ROWS + np.arange(NA_Q_ROWS)
        r_start = np.clip(qr - NA_ROWS // 2, 0, rows - NA_ROWS)
        row_ok = (kr[:, None] >= r_start[None, :]) & (kr[:, None] < r_start[None, :] + NA_ROWS)
        r_idx = np.clip(kr[:, None] - qr[None, :] + NA_ROWS - 1, 0, 2 * NA_ROWS - 2)
        bias = jnp.take(by_col, r_idx.reshape(-1), axis=1)
        bias = bias.reshape(n_heads, NA_K_ROWS, NA_Q_ROWS, GRID_W, GRID_W)
        ok = row_ok[:, :, None, None] & col_ok[None, None, :, :]
        bias = jnp.where(ok[None], bias, NEG_INF)
        tiles.append(jnp.transpose(bias, (0, 1, 3, 2, 4)).reshape(
            n_heads, NA_K_ROWS * GRID_W, NA_Q_ROWS * GRID_W))
    return jnp.stack(tiles, axis=1)


def _trunk(x3, w):
    batch, seq, d = x3.shape
    x = x3.reshape(batch * seq, d)
    depth = w["ffn_w_in"].shape[0]
    for i in range(depth):
        j, kind = divmod(i, N_MIXERS)
        x = _ffn(x, w["ffn_norm"][i, 0], w["ffn_w_in"][i, 0], w["ffn_w_out"][i, 0], w["final_norm"], False)
        if kind == 0:
            lambda_init = 0.8 - 0.6 * math.exp(-0.3 * i)
            qt, k, vt = _proj(x, w["mix_norm"][i], *w["a_qkv"][j])
            lamf = w["a_lambda"][j].astype(F32)
            lam = (jnp.exp(jnp.sum(lamf[0] * lamf[1])) - jnp.exp(jnp.sum(lamf[2] * lamf[3]))
                   + lambda_init).reshape(1)
            o = _attn_a(qt, k, vt, *w["t5"], lam, w["a_subln"][j], batch, seq, 1.0 - lambda_init)
            w_out = w["a_w_out"][j]
        elif kind == 1:
            cos_t, sin_t = _rope_tables(seq)
            qt, k, vt = _proj_b(x, w["mix_norm"][i], *w["b_qkv"][j], w["b_q_norm"][j], w["b_k_norm"][j],
                                cos_t, sin_t)
            o = _attn_b(qt, k, vt, batch, seq)
            w_out = w["b_w_out"][j]
        else:
            qt, k, vt = _proj(x, w["mix_norm"][i], *w["c_qkv"][j])
            o = _attn_c(qt, k, vt, w["c_biasmask"][j], batch, seq)
            w_out = w["c_w_out"][j]
        x = _out_proj(x, o, w_out)
        x = _ffn(x, w["ffn_norm"][i, 1], w["ffn_w_in"][i, 1], w["ffn_w_out"][i, 1], w["final_norm"],
                 i == depth - 1)
    return x.reshape(batch, seq, d)


def _split_qkv_t(w_qkv, n_q, n_k):
    wq = jnp.swapaxes(w_qkv[:, :, :n_q], 1, 2).astype(BF16)
    wk = w_qkv[:, :, n_q:n_q + n_k].astype(BF16)
    wv = jnp.swapaxes(w_qkv[:, :, n_q + n_k:], 1, 2).astype(BF16)
    return [(wq[j], wk[j], wv[j]) for j in range(w_qkv.shape[0])]


def kernel(x_prompt, x_sample, ffn_norm, ffn_w_in, ffn_w_out, mix_norm, a_w_qkv, a_lambda, a_subln, a_w_out,
           t5_table, b_w_qkv, b_q_norm, b_k_norm, b_w_out, c_w_qkv, c_rpb, c_w_out, final_norm):
    d = x_prompt.shape[-1]
    n_kv = (b_w_qkv.shape[-1] - d) // 2
    b_qkv = _split_qkv_t(b_w_qkv, d, n_kv)
    w = {
        "ffn_norm": ffn_norm.astype(F32)[:, :, None, :],
        "ffn_w_in": ffn_w_in.astype(BF16),
        "ffn_w_out": ffn_w_out.astype(BF16),
        "mix_norm": mix_norm.astype(F32)[:, None, :],
        "final_norm": final_norm.astype(F32)[None, :],
        "a_qkv": _split_qkv_t(a_w_qkv, d, d),
        "a_lambda": a_lambda,
        "a_subln": a_subln.astype(F32)[:, :, None],
        "a_w_out": a_w_out.astype(BF16),
        "t5": _t5_tiles(t5_table, ATTN_A_TILE),
        "b_qkv": [(wq, jnp.swapaxes(wk, 0, 1), wv) for wq, wk, wv in b_qkv],
        "b_q_norm": b_q_norm.astype(F32)[:, :, None],
        "b_k_norm": b_k_norm.astype(F32)[:, :, None],
        "b_w_out": b_w_out.astype(BF16),
        "c_qkv": _split_qkv_t(c_w_qkv, d, d),
        "c_biasmask": [_na_biasmask(c_rpb[j]) for j in range(c_rpb.shape[0])],
        "c_w_out": c_w_out.astype(BF16),
    }
    return (_trunk(x_prompt, w), _trunk(x_sample, w))
```

```python
import functools
import math

import numpy as np
import jax
import jax.numpy as jnp
from jax import lax
from jax.experimental import pallas as pl
from jax.experimental.pallas import tpu as pltpu

F32 = jnp.float32
BF16 = jnp.bfloat16

EPS = 1e-6
HEAD_DIM = 64
GRID_W = 64
NEG_INF = -1e30
N_MIXERS = 3
T5_BUCKETS = 32
T5_MAX_DIST = 128
ROPE_THETA = 10000.0
NA_ROWS = 8
NA_COLS = 16
LOG2E = math.log2(math.e)
Q_SCALE = HEAD_DIM ** -0.5 * LOG2E

V7X_VMEM_BYTES = 64 * 1024 * 1024
VMEM_LIMIT_BYTES = V7X_VMEM_BYTES - 8 * 1024 * 1024
MXU_DIM = 256
BF16_SUBLANES = 16
ROW_TILE = 512
FFN_CHUNK = MXU_DIM
ATTN_A_TILE = 512
PIPE_CHUNK = 256
GUARD_LOG2 = 64.0
T5_WINDOW = 4
ATTN_B_TQ = 256
ATTN_B_TK = 1024
NA_Q_ROWS = 8
NA_K_ROWS = 16


def _params(*sem):
    return pltpu.CompilerParams(dimension_semantics=sem, vmem_limit_bytes=VMEM_LIMIT_BYTES)


def _const_spec(shape):
    nd = len(shape)
    return pl.BlockSpec(shape, lambda *_: (0,) * nd, pipeline_mode=pl.Buffered(1))


def _rms(x, gain):
    ms = jnp.mean(x * x, axis=-1, keepdims=True)
    return x * lax.rsqrt(ms + EPS) * gain


def _ffn_kernel(x_ref, g_ref, wi_ref, wo_ref, fg_ref, o_ref, act_ref, *, d_ff, final):
    x = x_ref[...]
    xn = _rms(x, g_ref[...]).astype(BF16)
    for c in range(d_ff // FFN_CHUNK):
        lo = c * FFN_CHUNK
        gate = jnp.dot(xn, wi_ref[:, lo:lo + FFN_CHUNK], preferred_element_type=F32)
        up = jnp.dot(xn, wi_ref[:, d_ff + lo:d_ff + lo + FFN_CHUNK], preferred_element_type=F32)
        act_ref[:, lo:lo + FFN_CHUNK] = (gate * jax.nn.sigmoid(gate) * up).astype(BF16)
    y = x + 0.5 * jnp.dot(act_ref[...], wo_ref[...], preferred_element_type=F32)
    if final:
        y = _rms(y, fg_ref[...])
    o_ref[...] = y


def _out_ffn_kernel(x_ref, a_ref, wp_ref, g_ref, wi_ref, wo_ref, fg_ref, o_ref, act_ref, x1_ref, *, d_ff, final):
    x1_ref[...] = x_ref[...] + jnp.dot(a_ref[...], wp_ref[...], preferred_element_type=F32)
    _ffn_kernel(x1_ref, g_ref, wi_ref, wo_ref, fg_ref, o_ref, act_ref, d_ff=d_ff, final=final)


def _out_ffn(x, attn, w_proj, gain, w_in, w_out, final_gain, final):
    t, d = x.shape
    d_ff = w_out.shape[0]
    row = pl.BlockSpec((ROW_TILE, d), lambda i: (i, 0))
    return pl.pallas_call(
        functools.partial(_out_ffn_kernel, d_ff=d_ff, final=final),
        grid=(t // ROW_TILE,),
        in_specs=[row, row, _const_spec(w_proj.shape), _const_spec((1, d)), _const_spec(w_in.shape),
                  _const_spec(w_out.shape), _const_spec((1, d))],
        out_specs=row,
        out_shape=jax.ShapeDtypeStruct((t, d), F32),
        scratch_shapes=[pltpu.VMEM((ROW_TILE, d_ff), BF16), pltpu.VMEM((ROW_TILE, d), F32)],
        compiler_params=_params("parallel"),
        name="out_proj_ffn",
    )(x, attn, w_proj, gain, w_in, w_out, final_gain)


def _ffn(x, gain, w_in, w_out, final_gain, final):
    t, d = x.shape
    d_ff = w_out.shape[0]
    row = pl.BlockSpec((ROW_TILE, d), lambda i: (i, 0))
    return pl.pallas_call(
        functools.partial(_ffn_kernel, d_ff=d_ff, final=final),
        grid=(t // ROW_TILE,),
        in_specs=[row, _const_spec((1, d)), _const_spec(w_in.shape), _const_spec(w_out.shape),
                  _const_spec((1, d))],
        out_specs=row,
        out_shape=jax.ShapeDtypeStruct((t, d), F32),
        scratch_shapes=[pltpu.VMEM((ROW_TILE, d_ff), BF16)],
        compiler_params=_params("parallel"),
        name="ffn",
    )(x, gain, w_in, w_out, final_gain)


def _dot_nt(a, b):
    return lax.dot_general(a, b, (((1,), (1,)), ((), ())), preferred_element_type=F32)


def _proj_kernel(x_ref, g_ref, wqt_ref, wk_ref, wvt_ref, qt_ref, k_ref, vt_ref):
    hn = _rms(x_ref[...], g_ref[...]).astype(BF16)
    qt_ref[...] = (_dot_nt(wqt_ref[...], hn) * Q_SCALE).astype(BF16)
    k_ref[...] = jnp.dot(hn, wk_ref[...], preferred_element_type=F32).astype(BF16)
    vt_ref[...] = _dot_nt(wvt_ref[...], hn).astype(BF16)


def _proj(x, gain, wqt, wk, wvt):
    t, d = x.shape
    row = pl.BlockSpec((ROW_TILE, d), lambda i: (i, 0))
    col = pl.BlockSpec((d, ROW_TILE), lambda i: (0, i))
    return pl.pallas_call(
        _proj_kernel,
        grid=(t // ROW_TILE,),
        in_specs=[row, _const_spec((1, d)), _const_spec(wqt.shape), _const_spec(wk.shape),
                  _const_spec(wvt.shape)],
        out_specs=[col, row, col],
        out_shape=[jax.ShapeDtypeStruct((d, t), BF16), jax.ShapeDtypeStruct((t, d), BF16),
                   jax.ShapeDtypeStruct((d, t), BF16)],
        compiler_params=_params("parallel"),
        name="proj_qkv",
    )(x, gain, wqt, wk, wvt)


def _norm_rope_t(yt, gain, cos, sin, n_heads):
    q4 = HEAD_DIM // 4
    out = []
    for h in range(n_heads):
        y = yt[h * HEAD_DIM:(h + 1) * HEAD_DIM]
        ms = jnp.mean(y * y, axis=0, keepdims=True)
        yn = y * lax.rsqrt(ms + EPS) * gain
        swapped = jnp.concatenate([yn[q4:2 * q4], yn[:q4], yn[3 * q4:], yn[2 * q4:3 * q4]], axis=0)
        out.append(yn * cos + swapped * sin)
    return jnp.concatenate(out, axis=0)


def _proj_b_kernel(x_ref, g_ref, wqt_ref, wkt_ref, wvt_ref, qn_ref, kn_ref, cos_ref, sin_ref,
                   qt_ref, k_ref, vt_ref, *, n_q, n_kv):
    hn = _rms(x_ref[...], g_ref[...]).astype(BF16)
    cos = cos_ref[...]
    sin = sin_ref[...]
    q = _norm_rope_t(_dot_nt(wqt_ref[...], hn), qn_ref[...], cos, sin, n_q)
    qt_ref[...] = (q * Q_SCALE).astype(BF16)
    k = _norm_rope_t(_dot_nt(wkt_ref[...], hn), kn_ref[...], cos, sin, n_kv)
    k_ref[...] = k.T.astype(BF16)
    vt_ref[...] = _dot_nt(wvt_ref[...], hn).astype(BF16)


def _proj_b(x, gain, wqt, wkt, wvt, q_norm, k_norm, cos_t, sin_t):
    t, d = x.shape
    dq, dkv = wqt.shape[0], wkt.shape[0]
    s_tiles = cos_t.shape[1] // ROW_TILE
    row = pl.BlockSpec((ROW_TILE, d), lambda i: (i, 0))
    tab = pl.BlockSpec((HEAD_DIM, ROW_TILE), lambda i: (0, i % s_tiles))
    return pl.pallas_call(
        functools.partial(_proj_b_kernel, n_q=dq // HEAD_DIM, n_kv=dkv // HEAD_DIM),
        grid=(t // ROW_TILE,),
        in_specs=[row, _const_spec((1, d)), _const_spec(wqt.shape), _const_spec(wkt.shape),
                  _const_spec(wvt.shape), _const_spec((HEAD_DIM, 1)), _const_spec((HEAD_DIM, 1)),
                  tab, tab],
        out_specs=[pl.BlockSpec((dq, ROW_TILE), lambda i: (0, i)),
                   pl.BlockSpec((ROW_TILE, dkv), lambda i: (i, 0)),
                   pl.BlockSpec((dkv, ROW_TILE), lambda i: (0, i))],
        out_shape=[jax.ShapeDtypeStruct((dq, t), BF16), jax.ShapeDtypeStruct((t, dkv), BF16),
                   jax.ShapeDtypeStruct((dkv, t), BF16)],
        compiler_params=_params("parallel"),
        name="proj_qkv_rope",
    )(x, gain, wqt, wkt, wvt, q_norm, k_norm, cos_t, sin_t)


def _out_proj_kernel(x_ref, o_ref, w_ref, y_ref):
    y_ref[...] = x_ref[...] + jnp.dot(o_ref[...], w_ref[...], preferred_element_type=F32)


def _out_proj(x, o, w):
    t, d = x.shape
    row = pl.BlockSpec((ROW_TILE, d), lambda i: (i, 0))
    return pl.pallas_call(
        _out_proj_kernel,
        grid=(t // ROW_TILE,),
        in_specs=[row, row, _const_spec(w.shape)],
        out_specs=row,
        out_shape=jax.ShapeDtypeStruct((t, d), F32),
        compiler_params=_params("parallel"),
        name="out_proj",
    )(x, o, w)


class _Pipe:
    def __init__(self, k_ref, vt_ref, qm_ref, p0, p1, a0, a1, b0, b1, m_ref, acc_ref, g_ref, tk):
        self.k_ref, self.vt_ref, self.qm_ref = k_ref, vt_ref, qm_ref
        self.p, self.a, self.b = (p0, p1), (a0, a1), (b0, b1)
        self.m_ref, self.acc_ref, self.g_ref, self.tk = m_ref, acc_ref, g_ref, tk

    def reset(self):
        self.m_ref[...] = jnp.full(self.m_ref.shape, NEG_INF, F32)
        self.acc_ref[...] = jnp.zeros(self.acc_ref.shape, F32)

    def _chunks(self):
        n_cols = self.m_ref.shape[1]
        return [slice(j, j + PIPE_CHUNK) for j in range(0, n_cols, PIPE_CHUNK)]

    def _scores(self, k_tile, cs, bias):
        s = jnp.dot(k_tile, self.qm_ref[:, cs], preferred_element_type=F32)
        if bias is not None:
            b0 = cs.start % bias.shape[1]
            s = s + bias[:, b0:b0 + PIPE_CHUNK]
        return s

    def expo(self, slot, tile, bias=None, shift_const=None):
        k0 = pl.multiple_of(tile * self.tk, self.tk)
        k_tile = self.k_ref[pl.ds(k0, self.tk), :]
        jump = None
        for cs in self._chunks():
            s = self._scores(k_tile, cs, bias)
            m_prev = self.m_ref[:, cs]
            m_tile = jnp.max(s, axis=0, keepdims=True)
            if shift_const is not None:
                m_tile = m_tile + shift_const
                self.p[slot][:, cs] = jnp.exp2(s - (m_prev - shift_const)).astype(BF16)
            else:
                self.p[slot][:, cs] = jnp.exp2(s - m_prev).astype(BF16)
            m_new = jnp.maximum(m_prev, m_tile)
            alpha = jnp.exp2(m_prev - m_new)
            self.a[slot][:, cs] = alpha
            self.b[slot][:, cs] = alpha
            self.m_ref[:, cs] = m_new
            gap = jnp.max(m_tile - m_prev)
            jump = gap if jump is None else jnp.maximum(jump, gap)
        self.g_ref[slot] = (jump > GUARD_LOG2).astype(jnp.int32)

    def recompute(self, slot, tile, bias=None, shift_const=None):
        @pl.when(self.g_ref[slot] != 0)
        def _():
            k0 = pl.multiple_of(tile * self.tk, self.tk)
            k_tile = self.k_ref[pl.ds(k0, self.tk), :]
            for cs in self._chunks():
                s = self._scores(k_tile, cs, bias)
                m_new = self.m_ref[:, cs]
                if shift_const is not None:
                    m_new = m_new - shift_const
                self.p[slot][:, cs] = jnp.exp2(s - m_new).astype(BF16)
                self.b[slot][:, cs] = jnp.ones((1, PIPE_CHUNK), F32)
            self.g_ref[slot] = jnp.int32(0)

    def values(self, slot, tile):
        k0 = pl.multiple_of(tile * self.tk, self.tk)
        ones = jnp.ones((BF16_SUBLANES, self.tk), BF16)
        vt = jnp.concatenate([self.vt_ref[:, pl.ds(k0, self.tk)], ones], axis=0)
        for cs in self._chunks():
            pv = jnp.dot(vt, self.p[slot][:, cs], preferred_element_type=F32)
            self.acc_ref[:, cs] = self.a[slot][:, cs] * self.acc_ref[:, cs] + self.b[slot][:, cs] * pv


def _pipe_scratch(tk, n_cols, dv):
    return [pltpu.VMEM((2 * HEAD_DIM, n_cols), BF16),
            pltpu.VMEM((tk, n_cols), BF16), pltpu.VMEM((tk, n_cols), BF16),
            pltpu.VMEM((1, n_cols), F32), pltpu.VMEM((1, n_cols), F32),
            pltpu.VMEM((1, n_cols), F32), pltpu.VMEM((1, n_cols), F32),
            pltpu.VMEM((1, n_cols), F32),
            pltpu.VMEM((dv + BF16_SUBLANES, n_cols), F32),
            pltpu.SMEM((2,), jnp.int32)]


def _attn_a_kernel(lam_ref, far_ref, qt_ref, k_ref, vt_ref, bias_ref, subln_ref, o_ref,
                   qm_ref, p0, p1, a0, a1, b0, b1, m_ref, acc_ref, g_ref, *, seq, tile, out_scale):
    head = pl.program_id(0)
    n_tiles = seq // tile
    n_far = n_tiles - T5_WINDOW
    dv = 2 * HEAD_DIM
    lam = lam_ref[0]
    c_left = far_ref[head, 0]
    c_right = far_ref[head, 1]
    first_half = lax.broadcasted_iota(jnp.int32, (dv, tile), 0) < HEAD_DIM
    pipe = _Pipe(k_ref, vt_ref, qm_ref, p0, p1, a0, a1, b0, b1, m_ref, acc_ref, g_ref, tile)

    def q_body(qi, carry):
        q0 = pl.multiple_of(qi * tile, tile)
        q_tile = qt_ref[:, pl.ds(q0, tile)]
        zeros = jnp.zeros_like(q_tile)
        qm_ref[...] = jnp.concatenate([jnp.where(first_half, q_tile, zeros),
                                       jnp.where(first_half, zeros, q_tile)], axis=1)
        pipe.reset()
        w0 = jnp.clip(qi - 1, 0, n_far)

        def far_tile(f):
            f = jnp.minimum(f, n_far - 1)
            return jnp.where(f < w0, f, f + T5_WINDOW)

        def window_bias(i):
            return bias_ref.at[0, w0 + i - qi + (T5_WINDOW - 1)]

        for i in range(T5_WINDOW):
            if i > 0:
                pipe.recompute((i - 1) % 2, w0 + i - 1, bias=window_bias(i - 1))
            pipe.expo(i % 2, w0 + i, bias=window_bias(i))
            if i > 0:
                pipe.values((i - 1) % 2, w0 + i - 1)
        last_w = T5_WINDOW - 1
        pipe.recompute(last_w % 2, w0 + last_w, bias=window_bias(last_w))

        def far_step(f, c):
            prev = jnp.where(f == 0, w0 + last_w, far_tile(f - 1))
            prev_shift = jnp.where(f - 1 < w0, c_left, c_right)
            shift = jnp.where(f < w0, c_left, c_right)
            cur = far_tile(f)
            for u in range(2):
                @pl.when((f & 1) == u)
                def _():
                    pipe.recompute(1 - u, prev, shift_const=prev_shift)
                    pipe.expo(u, cur, shift_const=shift)
                    pipe.values(1 - u, prev)
            return c

        if n_far > 0:
            lax.fori_loop(0, n_far, far_step, 0)
            last = far_tile(n_far - 1)
            pipe.recompute(1, last, shift_const=jnp.where(n_far - 1 < w0, c_left, c_right))
            pipe.values(1, last)
        else:
            pipe.values(1, w0 + last_w)

        acc = acc_ref[...]
        acc = acc[:dv] * (1.0 / acc[dv:dv + 1])
        o = acc[:, :tile] - lam * acc[:, tile:]
        ms = jnp.mean(o * o, axis=0, keepdims=True)
        o = o * lax.rsqrt(ms + EPS) * subln_ref[...] * out_scale
        o_ref[pl.ds(q0, tile), :] = o.T.astype(BF16)
        return carry

    lax.fori_loop(0, n_tiles, q_body, 0)


def _attn_a(qt, k, vt, bias_tiles, far_const, lam, subln, batch, seq, out_scale):
    d = k.shape[1]
    hw = 2 * HEAD_DIM
    tile = bias_tiles.shape[-1]
    n_tiles = seq // tile
    assert n_tiles >= T5_WINDOW and n_tiles % 2 == 0
    smem = pl.BlockSpec(memory_space=pltpu.SMEM)
    return pl.pallas_call(
        functools.partial(_attn_a_kernel, seq=seq, tile=tile, out_scale=out_scale),
        grid=(d // hw, batch),
        in_specs=[smem, smem,
                  pl.BlockSpec((hw, seq), lambda h, b: (h, b)),
                  pl.BlockSpec((seq, hw), lambda h, b: (b, h)),
                  pl.BlockSpec((hw, seq), lambda h, b: (h, b)),
                  pl.BlockSpec((1, 2 * T5_WINDOW - 1, tile, tile), lambda h, b: (h, 0, 0, 0),
                               pipeline_mode=pl.Buffered(1)),
                  _const_spec((hw, 1))],
        out_specs=pl.BlockSpec((seq, hw), lambda h, b: (b, h)),
        out_shape=jax.ShapeDtypeStruct(k.shape, BF16),
        scratch_shapes=_pipe_scratch(tile, 2 * tile, hw),
        compiler_params=_params("parallel", "parallel"),
        name="attn_diff",
    )(lam, far_const, qt, k, vt, bias_tiles, subln)


def _t5_bias_vec(t5_table, span):
    nb = T5_BUCKETS // 2
    max_exact = nb // 2
    rel = jnp.arange(-(span - 1), span, dtype=jnp.int32)
    ret = jnp.where(rel > 0, nb, 0)
    n = jnp.abs(rel)
    n_f = jnp.maximum(n, 1).astype(F32)
    large = max_exact + (jnp.log(n_f / max_exact) / math.log(T5_MAX_DIST / max_exact)
                         * (nb - max_exact)).astype(jnp.int32)
    large = jnp.minimum(large, nb - 1)
    bucket = ret + jnp.where(n < max_exact, n, large)
    return t5_table.astype(F32)[bucket]


def _t5_tiles(t5_table, tile):
    assert tile > T5_MAX_DIST
    span = T5_WINDOW * tile
    vec_t = _t5_bias_vec(t5_table, span + 1).T * LOG2E
    period = 2 * tile
    y = np.arange(period)
    shift = np.where(y <= tile, -y, period - y)
    rel = np.stack([j * tile + shift for j in range(1 - T5_WINDOW, T5_WINDOW)])
    gen = vec_t[:, rel + span]
    flat = jnp.tile(gen, (1, 1, tile))[:, :, :tile * (period - 1)]
    tiles = flat.reshape(gen.shape[0], gen.shape[1], tile, period - 1)[:, :, :, :tile]
    far = jnp.stack([vec_t[:, 0], vec_t[:, -1]], axis=1)
    return tiles, far


def _attn_b_kernel(qt_ref, k_ref, vt_ref, o_ref, qm_ref, p0, p1, a0, a1, b0, b1, m_ref, acc_ref, g_ref,
                   *, seq, tq, tk, group):
    upper = pl.program_id(0) % 2 == 1
    n_tiles = seq // tk
    pipe = _Pipe(k_ref, vt_ref, qm_ref, p0, p1, a0, a1, b0, b1, m_ref, acc_ref, g_ref, tk)

    def q_body(qi, carry):
        q0 = pl.multiple_of(qi * tq, tq)
        q_tile = qt_ref[:, pl.ds(q0, tq)]
        q_cat = jnp.concatenate([q_tile[r * HEAD_DIM:(r + 1) * HEAD_DIM] for r in range(group)],
                                axis=1)
        zeros = jnp.zeros_like(q_cat)
        qm_ref[...] = jnp.concatenate([jnp.where(upper, zeros, q_cat),
                                       jnp.where(upper, q_cat, zeros)], axis=0)
        pipe.reset()
        pipe.expo(0, 0)

        def step(t, c):
            for u in range(2):
                @pl.when((t & 1) == u)
                def _():
                    pipe.recompute(1 - u, t - 1)
                    pipe.expo(u, t)
                    pipe.values(1 - u, t - 1)
            return c

        lax.fori_loop(1, n_tiles, step, 0)
        pipe.recompute(1, n_tiles - 1)
        pipe.values(1, n_tiles - 1)
        acc = acc_ref[...]
        o = acc[:HEAD_DIM] * (1.0 / acc[HEAD_DIM:HEAD_DIM + 1])
        o_cat = jnp.concatenate([o[:, r * tq:(r + 1) * tq] for r in range(group)], axis=0)
        o_ref[pl.ds(q0, tq), :] = o_cat.T.astype(BF16)
        return carry

    lax.fori_loop(0, seq // tq, q_body, 0)


def _attn_b(qt, k, vt, batch, seq):
    dq = qt.shape[0]
    n_kv = vt.shape[0] // HEAD_DIM
    group = dq // HEAD_DIM // n_kv
    gw = group * HEAD_DIM
    tq, tk = ATTN_B_TQ, ATTN_B_TK
    assert seq % (2 * tk) == 0 and seq % tq == 0
    return pl.pallas_call(
        functools.partial(_attn_b_kernel, seq=seq, tq=tq, tk=tk, group=group),
        grid=(n_kv, batch),
        in_specs=[pl.BlockSpec((gw, seq), lambda g, b: (g, b)),
                  pl.BlockSpec((seq, 2 * HEAD_DIM), lambda g, b: (b, g // 2)),
                  pl.BlockSpec((HEAD_DIM, seq), lambda g, b: (g, b))],
        out_specs=pl.BlockSpec((seq, gw), lambda g, b: (b, g)),
        out_shape=jax.ShapeDtypeStruct((batch * seq, dq), BF16),
        scratch_shapes=_pipe_scratch(tk, group * tq, HEAD_DIM),
        compiler_params=_params("parallel", "parallel"),
        name="attn_gqa",
    )(qt, k, vt)


def _rope_tables(seq):
    n = HEAD_DIM // 4
    t = jnp.arange(seq, dtype=jnp.int32)
    freqs = ROPE_THETA ** (-jnp.arange(n, dtype=F32) / n)

    def half(pos):
        ang = pos.astype(F32)[None, :] * freqs[:, None]
        c, s = jnp.cos(ang), jnp.sin(ang)
        return jnp.concatenate([c, c], axis=0), jnp.concatenate([-s, s], axis=0)

    c_row, s_row = half(t // GRID_W)
    c_col, s_col = half(t % GRID_W)
    return jnp.concatenate([c_row, c_col], axis=0), jnp.concatenate([s_row, s_col], axis=0)


def _attn_c_kernel(qt_ref, k_ref, vt_ref, bm_ref, o_ref, *, rows):
    j = pl.program_id(2)
    n_keys = NA_K_ROWS * GRID_W
    w0 = jnp.clip(j * NA_Q_ROWS - NA_ROWS // 2, 0, rows - NA_K_ROWS)
    k0 = pl.multiple_of(w0 * GRID_W, NA_ROWS // 2 * GRID_W)
    k_win = k_ref[pl.ds(k0, n_keys), :]
    q_tile = qt_ref[...]
    zeros = jnp.zeros_like(q_tile)
    first_head = lax.broadcasted_iota(jnp.int32, q_tile.shape, 0) < HEAD_DIM
    ones = jnp.ones((BF16_SUBLANES, n_keys), BF16)
    outs = []
    for hh in range(2):
        q_pad = jnp.where(first_head, q_tile, zeros) if hh == 0 else jnp.where(first_head, zeros, q_tile)
        s = jnp.dot(k_win, q_pad, preferred_element_type=F32) + bm_ref[hh, 0]
        p = jnp.exp2(s - jnp.max(s, axis=0, keepdims=True)).astype(BF16)
        v_win = jnp.concatenate([vt_ref[hh * HEAD_DIM:(hh + 1) * HEAD_DIM, pl.ds(k0, n_keys)], ones],
                                axis=0)
        acc = jnp.dot(v_win, p, preferred_element_type=F32)
        outs.append(acc[:HEAD_DIM] * (1.0 / acc[HEAD_DIM:HEAD_DIM + 1]))
    o_ref[...] = jnp.concatenate(outs, axis=0).T.astype(BF16)


def _attn_c(qt, k, vt, biasmask, batch, seq):
    d = k.shape[1]
    hw = 2 * HEAD_DIM
    rows = seq // GRID_W
    assert rows >= NA_K_ROWS and rows % NA_Q_ROWS == 0
    n_blk = rows // NA_Q_ROWS
    n_q = NA_Q_ROWS * GRID_W
    n_keys = NA_K_ROWS * GRID_W

    def variant(j):
        return jnp.where(j == 0, 0, jnp.where(j == n_blk - 1, 2, 1))

    return pl.pallas_call(
        functools.partial(_attn_c_kernel, rows=rows),
        grid=(d // hw, batch, n_blk),
        in_specs=[pl.BlockSpec((hw, n_q), lambda h, b, j: (h, b * n_blk + j)),
                  pl.BlockSpec((seq, hw), lambda h, b, j: (b, h)),
                  pl.BlockSpec((hw, seq), lambda h, b, j: (h, b)),
                  pl.BlockSpec((2, 1, n_keys, n_q), lambda h, b, j: (h, variant(j), 0, 0))],
        out_specs=pl.BlockSpec((n_q, hw), lambda h, b, j: (b * n_blk + j, h)),
        out_shape=jax.ShapeDtypeStruct(k.shape, BF16),
        compiler_params=_params("parallel", "parallel", "arbitrary"),
        name="attn_nbr",
    )(qt, k, vt, biasmask)


def _na_biasmask(rpb):
    rows = 4 * NA_Q_ROWS
    n_heads = rpb.shape[0]
    col = np.arange(GRID_W)
    c_start = np.clip(col - NA_COLS // 2, 0, GRID_W - NA_COLS)
    col_ok = (col[:, None] >= c_start[None, :]) & (col[:, None] < c_start[None, :] + NA_COLS)
    c_idx = np.clip(col[:, None] - col[None, :] + NA_COLS - 1, 0, 2 * NA_COLS - 2)
    by_col = jnp.take(rpb.astype(F32) * LOG2E, c_idx.reshape(-1), axis=2)
    n_blk = rows // NA_Q_ROWS
    tiles = []
    for j in (0, 1, n_blk - 1):
        w0 = min(max(j * NA_Q_ROWS - NA_ROWS // 2, 0), rows - NA_K_ROWS)
        kr = w0 + np.arange(NA_K_ROWS)
        qr = j * NA_Q_ROWS + np.arange(NA_Q_ROWS)
        r_start = np.clip(qr - NA_ROWS // 2, 0, rows - NA_ROWS)
        row_ok = (kr[:, None] >= r_start[None, :]) & (kr[:, None] < r_start[None, :] + NA_ROWS)
        r_idx = np.clip(kr[:, None] - qr[None, :] + NA_ROWS - 1, 0, 2 * NA_ROWS - 2)
        bias = jnp.take(by_col, r_idx.reshape(-1), axis=1)
        bias = bias.reshape(n_heads, NA_K_ROWS, NA_Q_ROWS, GRID_W, GRID_W)
        ok = row_ok[:, :, None, None] & col_ok[None, None, :, :]
        bias = jnp.where(ok[None], bias, NEG_INF)
        tiles.append(jnp.transpose(bias, (0, 1, 3, 2, 4)).reshape(
            n_heads, NA_K_ROWS * GRID_W, NA_Q_ROWS * GRID_W))
    return jnp.stack(tiles, axis=1)


def _trunk(x3, w):
    batch, seq, d = x3.shape
    x = x3.reshape(batch * seq, d)
    depth = w["ffn_w_in"].shape[0]
    for i in range(depth):
        j, kind = divmod(i, N_MIXERS)
        x = _ffn(x, w["ffn_norm"][i, 0], w["ffn_w_in"][i, 0], w["ffn_w_out"][i, 0], w["final_norm"], False)
        if kind == 0:
            lambda_init = 0.8 - 0.6 * math.exp(-0.3 * i)
            qt, k, vt = _proj(x, w["mix_norm"][i], *w["a_qkv"][j])
            lamf = w["a_lambda"][j].astype(F32)
            lam = (jnp.exp(jnp.sum(lamf[0] * lamf[1])) - jnp.exp(jnp.sum(lamf[2] * lamf[3]))
                   + lambda_init).reshape(1)
            o = _attn_a(qt, k, vt, *w["t5"], lam, w["a_subln"][j], batch, seq, 1.0 - lambda_init)
            w_out = w["a_w_out"][j]
        elif kind == 1:
            cos_t, sin_t = _rope_tables(seq)
            qt, k, vt = _proj_b(x, w["mix_norm"][i], *w["b_qkv"][j], w["b_q_norm"][j], w["b_k_norm"][j],
                                cos_t, sin_t)
            o = _attn_b(qt, k, vt, batch, seq)
            w_out = w["b_w_out"][j]
        else:
            qt, k, vt = _proj(x, w["mix_norm"][i], *w["c_qkv"][j])
            o = _attn_c(qt, k, vt, w["c_biasmask"][j], batch, seq)
            w_out = w["c_w_out"][j]
        x = _out_ffn(x, o, w_out, w["ffn_norm"][i, 1], w["ffn_w_in"][i, 1], w["ffn_w_out"][i, 1],
                     w["final_norm"], i == depth - 1)
    return x.reshape(batch, seq, d)


def _split_qkv_t(w_qkv, n_q, n_k):
    wq = jnp.swapaxes(w_qkv[:, :, :n_q], 1, 2).astype(BF16)
    wk = w_qkv[:, :, n_q:n_q + n_k].astype(BF16)
    wv = jnp.swapaxes(w_qkv[:, :, n_q + n_k:], 1, 2).astype(BF16)
    return [(wq[j], wk[j], wv[j]) for j in range(w_qkv.shape[0])]


def kernel(x_prompt, x_sample, ffn_norm, ffn_w_in, ffn_w_out, mix_norm, a_w_qkv, a_lambda, a_subln, a_w_out,
           t5_table, b_w_qkv, b_q_norm, b_k_norm, b_w_out, c_w_qkv, c_rpb, c_w_out, final_norm):
    d = x_prompt.shape[-1]
    n_kv = (b_w_qkv.shape[-1] - d) // 2
    b_qkv = _split_qkv_t(b_w_qkv, d, n_kv)
    w = {
        "ffn_norm": ffn_norm.astype(F32)[:, :, None, :],
        "ffn_w_in": ffn_w_in.astype(BF16),
        "ffn_w_out": ffn_w_out.astype(BF16),
        "mix_norm": mix_norm.astype(F32)[:, None, :],
        "final_norm": final_norm.astype(F32)[None, :],
        "a_qkv": _split_qkv_t(a_w_qkv, d, d),
        "a_lambda": a_lambda,
        "a_subln": a_subln.astype(F32)[:, :, None],
        "a_w_out": a_w_out.astype(BF16),
        "t5": _t5_tiles(t5_table, ATTN_A_TILE),
        "b_qkv": [(wq, jnp.swapaxes(wk, 0, 1), wv) for wq, wk, wv in b_qkv],
        "b_q_norm": b_q_norm.astype(F32)[:, :, None],
        "b_k_norm": b_k_norm.astype(F32)[:, :, None],
        "b_w_out": b_w_out.astype(BF16),
        "c_qkv": _split_qkv_t(c_w_qkv, d, d),
        "c_biasmask": [_na_biasmask(c_rpb[j]) for j in range(c_rpb.shape[0])],
        "c_w_out": c_w_out.astype(BF16),
    }
    return (_trunk(x_prompt, w), _trunk(x_sample, w))
```
